```python
import jax, jax.numpy as jnp
from jax import lax
import numpy as np

D_MODEL = 1024
BATCH = 16
SEQ = 256
DEPTH = 2
DEC_BATCH = 8
DEC_SEQ = 4096
PAST_LEN = 512

GRID_W = 64
N_DIR = 2
N_ADA = 6
RMS_EPS = 1e-6
N_A_LAYERS = (DEPTH + 1) // 2
N_B_LAYERS = DEPTH // 2
A_HEADS = 8
A_DK = D_MODEL // 16
A_DV = D_MODEL // A_HEADS
A_CHUNK = 64
A_QK = A_HEADS * A_DK
A_V = A_HEADS * A_DV
A_COLS = 2 * A_QK + 2 * A_V + 2 * N_DIR * A_HEADS
B_HEAD_DIM = 64
B_HEADS = D_MODEL // B_HEAD_DIM
B_LORA_W = 64
B_LORA_A = 64
B_LORA_G = 128
B_DECAY_SCALE = 0.606531
B_LN_EPS = 64e-5
D_FF = 128 * ((8 * D_MODEL // 3 + 127) // 128)
N_EXPERTS = 8
TOP_K = 2
D_FF_EXPERT = D_FF // 2

kernel_name = 'bidir_mlstm_rwkv7_prefix_diffusion_step'


def rmsnorm(x, g):
    xf = x.astype(jnp.float32)
    y = xf * lax.rsqrt(jnp.mean(xf * xf, -1, keepdims=True) + RMS_EPS)
    return y.astype(x.dtype) * g


def swiglu(h, w_gu, w_down):
    gate, up = jnp.split(h @ w_gu, 2, axis=-1)
    return (jax.nn.silu(gate) * up) @ w_down


def moe_swiglu(h, w_router, b_router, w_gu, w_down):
    B, T, D = h.shape
    hf = h.reshape(B * T, D)
    logits = (hf @ w_router).astype(jnp.float32) + b_router.astype(jnp.float32)
    top_logit, top_idx = lax.top_k(logits, TOP_K)
    top_w = jax.nn.softmax(top_logit, axis=-1)
    combine = jnp.sum(jax.nn.one_hot(top_idx, N_EXPERTS, dtype=jnp.float32) * top_w[..., None], axis=1).astype(h.dtype)
    out = jnp.zeros_like(hf)
    for e in range(N_EXPERTS):
        out = out + combine[:, e:e + 1] * swiglu(hf, w_gu[e], w_down[e])
    return out.reshape(B, T, D)


def shift_1d(x):
    half = x.shape[-1] // 2
    prev = jnp.pad(x[:, :-1, :half], ((0, 0), (1, 0), (0, 0)))
    nxt = jnp.pad(x[:, 1:, half:], ((0, 0), (0, 1), (0, 0)))
    return jnp.concatenate([prev, nxt], axis=-1)


def shift_grid(x):
    B, T, D = x.shape
    rows = T // GRID_W
    g = x.reshape(B, rows, GRID_W, D)
    q = D // 4
    left = jnp.pad(g[:, :, :-1, :q], ((0, 0), (0, 0), (1, 0), (0, 0)))
    right = jnp.pad(g[:, :, 1:, q:2 * q], ((0, 0), (0, 0), (0, 1), (0, 0)))
    up = jnp.pad(g[:, :-1, :, 2 * q:3 * q], ((0, 0), (1, 0), (0, 0), (0, 0)))
    down = jnp.pad(g[:, 1:, :, 3 * q:], ((0, 0), (0, 1), (0, 0), (0, 0)))
    return jnp.concatenate([left, right, up, down], axis=-1).reshape(B, T, D)


def mlstm_scan(q, k, v, log_i, log_f, C0, n0, m0):
    B, H, T, _ = q.shape
    nc = T // A_CHUNK

    def chunks(a):
        a = a.reshape((B, H, nc, A_CHUNK) + a.shape[3:])
        return jnp.moveaxis(a, 2, 0)

    tril = jnp.tril(jnp.ones((A_CHUNK, A_CHUNK), dtype=bool))

    def step(carry, inp):
        C, n, m = carry
        qc, kc, vc, ic, fc = inp
        b = jnp.cumsum(fc, axis=-1)
        logw = jnp.where(tril, b[..., :, None] - b[..., None, :] + ic[..., None, :], -jnp.inf)
        inter = b + m[..., None]
        m_row = jnp.maximum(inter, jnp.max(logw, axis=-1))
        w_intra = jnp.exp(logw - m_row[..., None])
        w_inter = jnp.exp(inter - m_row)
        s = jnp.einsum('bhjd,bhsd->bhjs', qc, kc) * w_intra
        num = jnp.einsum('bhjs,bhsv->bhjv', s, vc) + w_inter[..., None] * jnp.einsum('bhjd,bhdv->bhjv', qc, C)
        den = jnp.sum(s, axis=-1) + w_inter * jnp.einsum('bhjd,bhd->bhj', qc, n)
        h = num / jnp.maximum(jnp.abs(den), jnp.exp(-m_row))[..., None]
        b_end = b[..., -1]
        log_g = b_end[..., None] - b + ic
        m_new = jnp.maximum(b_end + m, jnp.max(log_g, axis=-1))
        decay = jnp.exp(b_end + m - m_new)
        w_s = jnp.exp(log_g - m_new[..., None])
        C = decay[..., None, None] * C + jnp.einsum('bhs,bhsd,bhsv->bhdv', w_s, kc, vc)
        n = decay[..., None] * n + jnp.einsum('bhs,bhsd->bhd', w_s, kc)
        return (C, n, m_new), h

    (C, n, m), h = lax.scan(step, (C0, n0, m0), (chunks(q), chunks(k), chunks(v), chunks(log_i), chunks(log_f)))
    h = jnp.moveaxis(h, 0, 2).reshape(B, H, T, -1)
    return h, (C, n, m)


def mlstm_mixer(h, w_in, b_in, g_head, w_out, C0, n0, m0):
    B, T, _ = h.shape
    f32 = jnp.float32
    z = h @ w_in + b_in
    cuts = [A_QK, 2 * A_QK, 2 * A_QK + A_V, 2 * A_QK + 2 * A_V, 2 * A_QK + 2 * A_V + N_DIR * A_HEADS]
    q, k, v, o, gi, gf = jnp.split(z, cuts, axis=-1)

    def heads(a, d):
        return a.reshape(B, T, A_HEADS, d).transpose(0, 2, 1, 3).astype(f32)

    q = heads(q, A_DK)
    k = heads(k, A_DK) * (A_DK ** -0.5)
    v = heads(v, A_DV)
    log_i = gi.astype(f32).reshape(B, T, N_DIR, A_HEADS).transpose(2, 0, 3, 1)
    log_f = jax.nn.log_sigmoid(gf.astype(f32).reshape(B, T, N_DIR, A_HEADS).transpose(2, 0, 3, 1))
    C0, n0, m0 = C0.astype(f32), n0.astype(f32), m0.astype(f32)
    h_f, st_f = mlstm_scan(q, k, v, log_i[0], log_f[0], C0[:, 0], n0[:, 0], m0[:, 0])
    flip = lambda a: jnp.flip(a, axis=2)
    h_b, st_b = mlstm_scan(flip(q), flip(k), flip(v), flip(log_i[1]), flip(log_f[1]), C0[:, 1], n0[:, 1], m0[:, 1])
    hs = h_f + flip(h_b)
    hs = hs * lax.rsqrt(jnp.mean(hs * hs, axis=-1, keepdims=True) + RMS_EPS)
    hs = hs.transpose(0, 2, 1, 3).reshape(B, T, A_V).astype(h.dtype) * g_head
    out = (hs * jax.nn.sigmoid(o)) @ w_out
    C = jnp.stack([st_f[0], st_b[0]], axis=1)
    n = jnp.stack([st_f[1], st_b[1]], axis=1)
    m = jnp.stack([st_f[2], st_b[2]], axis=1)
    return out, (C, n, m)


def rwkv_scan(r, w, k, v, kk, a, S0):
    def step(S, inp):
        r_t, w_t, k_t, v_t, kk_t, a_t = inp
        S = (S * w_t[:, :, None, :]
             - jnp.einsum('bhvk,bhk->bhv', S, kk_t)[..., None] * (kk_t * a_t)[:, :, None, :]
             + v_t[..., None] * k_t[:, :, None, :])
        return S, jnp.einsum('bhvk,bhk->bhv', S, r_t)
    return lax.scan(step, S0, (r, w, k, v, kk, a))


def rwkv_mixer(h, mu, w_rkv, w0, w1, w2, a0, a1, a2, g1, g2, k_k, k_a, r_k, ln_g, ln_b, w_out, S0, grid):
    B, T, D = h.shape
    f32 = jnp.float32
    xx = (shift_grid(h) if grid else shift_1d(h)) - h
    xm = h[None] + xx[None] * mu[:, None, None, :]
    rkv = jnp.einsum('cbtd,cde->cbte', xm[:3], w_rkv).astype(f32)
    r, k, v = rkv[0], rkv[1], rkv[2]
    xw, xa, xg = xm[3], xm[4], xm[5]
    g = jax.nn.sigmoid(xg @ g1) @ g2
    lora_w = jnp.einsum('zbtl,zld->zbtd', jnp.tanh(jnp.einsum('btd,zdl->zbtl', xw, w1)), w2)
    log_w = -B_DECAY_SCALE * jax.nn.sigmoid((w0[:, None, None, :] + lora_w).astype(f32))
    lora_a = jnp.einsum('zbtl,zld->zbtd', jnp.einsum('btd,zdl->zbtl', xa, a1), a2)
    a = jax.nn.sigmoid((a0[:, None, None, :] + lora_a).astype(f32))
    heads = lambda t: t.reshape(t.shape[:-1] + (B_HEADS, B_HEAD_DIM))
    kk = heads(k * k_k.astype(f32))
    kk = kk / jnp.maximum(jnp.sqrt(jnp.sum(kk * kk, axis=-1, keepdims=True)), 1e-12)
    k_dir = heads(k[None] * (1.0 + (a - 1.0) * k_a.astype(f32)))
    r_h, v_h, a_h, w_h = heads(r), heads(v), heads(a), heads(jnp.exp(log_w))
    tm = lambda t: jnp.swapaxes(t, 0, 1)
    flip = lambda t: jnp.flip(t, axis=0)
    S0 = S0.astype(f32)
    S_f, y_f = rwkv_scan(tm(r_h), tm(w_h[0]), tm(k_dir[0]), tm(v_h), tm(kk), tm(a_h[0]), S0[:, 0])
    S_b, y_b = rwkv_scan(flip(tm(r_h)), flip(tm(w_h[1])), flip(tm(k_dir[1])), flip(tm(v_h)), flip(tm(kk)), flip(tm(a_h[1])), S0[:, 1])
    y = tm(y_f) + tm(flip(y_b))
    bonus = jnp.sum(jnp.sum(r_h[None] * k_dir * r_k.astype(f32), axis=-1, keepdims=True) * v_h[None], axis=0)
    y = y + bonus
    mean = jnp.mean(y, axis=-1, keepdims=True)
    var = jnp.mean(jnp.square(y - mean), axis=-1, keepdims=True)
    y = ((y - mean) * lax.rsqrt(var + B_LN_EPS)).reshape(B, T, D).astype(h.dtype) * ln_g + ln_b
    out = (y * g) @ w_out
    return out, jnp.stack([S_f, S_b], axis=1)


def setup_inputs(seed: int = 0) -> dict:
    key = jax.random.key(seed)
    ks = iter(jax.random.split(key, 64))
    nrm = lambda shape, scale: jax.random.normal(next(ks), shape, jnp.float32) * scale
    D = D_MODEL
    NA, NB = N_A_LAYERS, N_B_LAYERS
    f_bias = jnp.tile(jnp.linspace(3.0, 6.0, A_HEADS), N_DIR)
    b_m_in = nrm((NA, A_COLS), 0.02).at[:, A_COLS - N_DIR * A_HEADS:].add(f_bias)
    return {
        'x_prompt': nrm((BATCH, SEQ, D), 1.0),
        'x_sample': nrm((DEC_BATCH, DEC_SEQ, D), 1.0),
        'state_mlstm_C': nrm((DEC_BATCH, NA, N_DIR, A_HEADS, A_DK, A_DV), 0.05),
        'state_mlstm_n': nrm((DEC_BATCH, NA, N_DIR, A_HEADS, A_DK), 0.1),
        'state_mlstm_m': nrm((DEC_BATCH, NA, N_DIR, A_HEADS), 1.0),
        'state_rwkv_S': nrm((DEC_BATCH, NB, N_DIR, B_HEADS, B_HEAD_DIM, B_HEAD_DIM), 0.1),
        'c': nrm((DEC_BATCH, D), 1.0),
        'c_ctx': nrm((D,), 1.0),
        'w_ada': nrm((DEPTH, D, N_ADA * D), 0.5 * D ** -0.5),
        'b_ada': nrm((DEPTH, N_ADA * D), 0.02),
        'g_norm1': 1.0 + nrm((DEPTH, D), 0.02),
        'g_norm2': 1.0 + nrm((DEPTH, D), 0.02),
        'w_m_in': nrm((NA, D, A_COLS), D ** -0.5),
        'b_m_in': b_m_in,
        'g_m_head': 1.0 + nrm((NA, A_V), 0.02),
        'w_m_out': nrm((NA, A_V, D), A_V ** -0.5),
        'mu_rwkv': jax.random.uniform(next(ks), (NB, 6, D), jnp.float32),
        'w_rkv': nrm((NB, 3, D, D), D ** -0.5),
        'w_decay0': jax.random.uniform(next(ks), (NB, N_DIR, D), jnp.float32, -3.0, 3.0),
        'w_decay1': nrm((NB, N_DIR, D, B_LORA_W), D ** -0.5),
        'w_decay2': nrm((NB, N_DIR, B_LORA_W, D), 0.5 * B_LORA_W ** -0.5),
        'w_iclr0': nrm((NB, N_DIR, D), 0.5),
        'w_iclr1': nrm((NB, N_DIR, D, B_LORA_A), D ** -0.5),
        'w_iclr2': nrm((NB, N_DIR, B_LORA_A, D), 0.5 * B_LORA_A ** -0.5),
        'w_gate1': nrm((NB, D, B_LORA_G), D ** -0.5),
        'w_gate2': nrm((NB, B_LORA_G, D), B_LORA_G ** -0.5),
        'k_k': 0.85 + nrm((NB, D), 0.02),
        'k_a': 1.0 + nrm((NB, D), 0.02),
        'r_k': nrm((NB, B_HEADS, B_HEAD_DIM), 0.1),
        'g_ln_x': 1.0 + nrm((NB, D), 0.02),
        'b_ln_x': nrm((NB, D), 0.02),
        'w_rwkv_out': nrm((NB, D, D), D ** -0.5),
        'w_ff_gu': nrm((NA, D, 2 * D_FF), D ** -0.5),
        'w_ff_down': nrm((NA, D_FF, D), D_FF ** -0.5),
        'w_router': nrm((NB, D, N_EXPERTS), D ** -0.5),
        'b_router': nrm((NB, N_EXPERTS), 0.01),
        'w_exp_gu': nrm((NB, N_EXPERTS, D, 2 * D_FF_EXPERT), D ** -0.5),
        'w_exp_down': nrm((NB, N_EXPERTS, D_FF_EXPERT, D), D_FF_EXPERT ** -0.5),
        'g_final': 1.0 + nrm((D,), 0.02),
    }


def reference(x_prompt, x_sample, state_mlstm_C, state_mlstm_n, state_mlstm_m, state_rwkv_S, c, c_ctx,
              w_ada, b_ada, g_norm1, g_norm2, w_m_in, b_m_in, g_m_head, w_m_out,
              mu_rwkv, w_rkv, w_decay0, w_decay1, w_decay2, w_iclr0, w_iclr1, w_iclr2,
              w_gate1, w_gate2, k_k, k_a, r_k, g_ln_x, b_ln_x, w_rwkv_out,
              w_ff_gu, w_ff_down, w_router, b_router, w_exp_gu, w_exp_down, g_final):

    def layer_stack(x, cond, init_C, init_n, init_m, init_S, grid):
        new_A, new_B = [], []
        for i in range(DEPTH):
            j = i // 2
            mod = (jax.nn.silu(cond) @ w_ada[i] + b_ada[i]).reshape(cond.shape[0], N_ADA, 1, D_MODEL)
            sh1, sc1, ga1, sh2, sc2, ga2 = (mod[:, t] for t in range(N_ADA))
            hn = rmsnorm(x, g_norm1[i]) * (1.0 + sc1) + sh1
            if i % 2 == 0:
                out, st = mlstm_mixer(hn, w_m_in[j], b_m_in[j], g_m_head[j], w_m_out[j],
                                      init_C[:, j], init_n[:, j], init_m[:, j])
                new_A.append(st)
            else:
                out, st = rwkv_mixer(hn, mu_rwkv[j], w_rkv[j], w_decay0[j], w_decay1[j], w_decay2[j],
                                     w_iclr0[j], w_iclr1[j], w_iclr2[j], w_gate1[j], w_gate2[j],
                                     k_k[j], k_a[j], r_k[j], g_ln_x[j], b_ln_x[j], w_rwkv_out[j],
                                     init_S[:, j], grid)
                new_B.append(st)
            x = x + ga1 * out
            hn = rmsnorm(x, g_norm2[i]) * (1.0 + sc2) + sh2
            if i % 2 == 0:
                ff = swiglu(hn, w_ff_gu[j], w_ff_down[j])
            else:
                ff = moe_swiglu(hn, w_router[j], b_router[j], w_exp_gu[j], w_exp_down[j])
            x = x + ga2 * ff
        return rmsnorm(x, g_final), new_A, new_B

    nb = x_prompt.shape[0]
    zC = jnp.zeros((nb, N_A_LAYERS, N_DIR, A_HEADS, A_DK, A_DV), jnp.float32)
    zn = jnp.zeros((nb, N_A_LAYERS, N_DIR, A_HEADS, A_DK), jnp.float32)
    zm = jnp.zeros((nb, N_A_LAYERS, N_DIR, A_HEADS), jnp.float32)
    zS = jnp.zeros((nb, N_B_LAYERS, N_DIR, B_HEADS, B_HEAD_DIM, B_HEAD_DIM), jnp.float32)
    y_prompt, ctx_A, ctx_B = layer_stack(x_prompt, c_ctx[None, :], zC, zn, zm, zS, False)
    y_sample, _, _ = layer_stack(x_sample, c, state_mlstm_C, state_mlstm_n, state_mlstm_m, state_rwkv_S, True)
    new_mlstm_C = jnp.stack([s[0] for s in ctx_A], axis=1)
    new_mlstm_n = jnp.stack([s[1] for s in ctx_A], axis=1)
    new_mlstm_m = jnp.stack([s[2] for s in ctx_A], axis=1)
    new_rwkv_S = jnp.stack(ctx_B, axis=1)
    return (y_prompt, y_sample, new_mlstm_C, new_mlstm_n, new_mlstm_m, new_rwkv_S)
```

```python
import functools

import jax
import jax.numpy as jnp
from jax import lax
from jax.experimental import pallas as pl
from jax.experimental.pallas import tpu as pltpu

F32 = jnp.float32
BF16 = jnp.bfloat16
HIGHEST = lax.Precision.HIGHEST

D_MODEL = 1024
GROUP_ROWS = 4096
N_ADA = 6
RMS_EPS = 1e-6
GRID_W = 64
A_HEADS = 8
A_DK = 64
A_DV = 128
A_QK = A_HEADS * A_DK
A_V = A_HEADS * A_DV
A_CHUNK = 128
A_GATE_COLS = 128
B_HEADS = 16
B_HEAD_DIM = 64
B_PAIRS = B_HEADS // 2
B_PAIR_W = 2 * B_HEAD_DIM
B_CHUNK = 64
B_DECAY_SCALE = 0.606531
B_LN_EPS = 64e-5
D_FF = 2816
N_EXPERTS = 8
D_FF_EXPERT = D_FF // 2
ROUTER_COLS = 128
NEG_BIG = -1e30

VMEM_LIMIT = 56 * 1024 * 1024


def _cparams(*sem):
    return pltpu.CompilerParams(dimension_semantics=sem, vmem_limit_bytes=VMEM_LIMIT)


def _dot(a, b, precision=None):
    return jnp.dot(a, b, precision=precision, preferred_element_type=F32)


def _dot_nt(a, b, precision=None):
    return lax.dot_general(a, b, (((1,), (1,)), ((), ())), precision=precision, preferred_element_type=F32)


def _dot_tn(a, b, precision=None):
    return lax.dot_general(a, b, (((0,), (0,)), ((), ())), precision=precision, preferred_element_type=F32)


def _norm_mod(x, g, sc, sh):
    y = x * lax.rsqrt(jnp.mean(x * x, axis=-1, keepdims=True) + RMS_EPS)
    return (y * g) * (1.0 + sc) + sh


def _log_sigmoid(x):
    return jnp.minimum(x, 0.0) - jnp.log1p(jnp.exp(-jnp.abs(x)))


def _ada_kernel(cond_ref, w_ref, b_ref, out_ref):
    c = cond_ref[...]
    out_ref[...] = _dot(c * jax.nn.sigmoid(c), w_ref[...], HIGHEST) + b_ref[...]


def _ada(cond_pad, w_ada, b_ada):
    depth, d, cols = w_ada.shape
    rows = cond_pad.shape[0]
    tn = 1536
    return pl.pallas_call(
        _ada_kernel,
        grid=(depth, cols // tn),
        in_specs=[
            pl.BlockSpec((rows, d), lambda l, j: (0, 0)),
            pl.BlockSpec((None, d, tn), lambda l, j: (l, 0, j)),
            pl.BlockSpec((None, 1, tn), lambda l, j: (l, 0, j)),
        ],
        out_specs=pl.BlockSpec((None, rows, tn), lambda l, j: (l, 0, j)),
        out_shape=jax.ShapeDtypeStruct((depth, rows, cols), F32),
        compiler_params=_cparams("parallel", "parallel"),
        name="ada",
    )(cond_pad, w_ada, b_ada.reshape(depth, 1, cols))


def _mlstm_scan_kernel(q_ref, k_ref, v_ref, gc_ref, gr_ref, c0_ref, n0_ref, m0_ref,
                       h_ref, cf_ref, nf_ref, mf_ref, c_sc, n_sc, m_sc, *, reverse, n_sub):
    L = A_CHUNK

    @pl.when(pl.program_id(1) == 0)
    def _():
        c_sc[...] = c0_ref[...]
        n_sc[...] = n0_ref[...]
        m_sc[...] = m0_ref[...]

    row = lax.broadcasted_iota(jnp.int32, (L, L), 0)
    col = lax.broadcasted_iota(jnp.int32, (L, L), 1)
    in_win = (col >= row) if reverse else (col <= row)
    in_win_t = (row >= col) if reverse else (row <= col)
    d0 = A_HEADS if reverse else 0

    subs = range(n_sub - 1, -1, -1) if reverse else range(n_sub)
    for sub in subs:
        r0 = sub * L
        gcol = gc_ref[r0:r0 + L, :]
        grow = gr_ref[:, r0:r0 + L]
        fcol_all = _log_sigmoid(gcol)
        frow_all = _log_sigmoid(grow)
        for h in range(A_HEADS):
            ci = d0 + h
            cf = 2 * A_HEADS + d0 + h
            i_col = gcol[:, ci:ci + 1]
            f_col = fcol_all[:, cf:cf + 1]
            i_row = grow[ci:ci + 1, :]
            f_row = frow_all[cf:cf + 1, :]
            b_col = jnp.sum(jnp.where(in_win, f_row, 0.0), axis=1, keepdims=True)
            b_row = jnp.sum(jnp.where(in_win_t, f_col, 0.0), axis=0, keepdims=True)
            tot = jnp.sum(f_row, axis=1, keepdims=True)
            m_prev = m_sc[h:h + 1, :]
            logw = jnp.where(in_win, b_col - b_row + i_row, -jnp.inf)
            inter = b_col + m_prev
            m_row = jnp.maximum(inter, jnp.max(logw, axis=1, keepdims=True))
            w_intra = jnp.exp(logw - m_row)
            w_inter = jnp.exp(inter - m_row)
            qh = q_ref[r0:r0 + L, h * A_DK:(h + 1) * A_DK]
            kh = k_ref[r0:r0 + L, h * A_DK:(h + 1) * A_DK] * (A_DK ** -0.5)
            vh = v_ref[r0:r0 + L, h * A_DV:(h + 1) * A_DV]
            c_prev = c_sc[h]
            n_prev = n_sc[h:h + 1, :]
            s = _dot_nt(qh, kh) * w_intra
            num = _dot(s, vh) + w_inter * _dot(qh, c_prev)
            den = jnp.sum(s, axis=1, keepdims=True) + w_inter * jnp.sum(qh * n_prev, axis=1, keepdims=True)
            h_ref[r0:r0 + L, h * A_DV:(h + 1) * A_DV] = num / jnp.maximum(jnp.abs(den), jnp.exp(-m_row))
            log_g = tot - b_col + i_col
            m_new = jnp.maximum(tot + m_prev, jnp.max(log_g, axis=0, keepdims=True))
            decay = jnp.exp(tot + m_prev - m_new)
            kw = kh * jnp.exp(log_g - m_new)
            c_sc[h] = decay * c_prev + _dot_tn(kw, vh)
            n_sc[h:h + 1, :] = decay * n_prev + jnp.sum(kw, axis=0, keepdims=True)
            m_sc[h:h + 1, :] = m_new

    cf_ref[...] = c_sc[...]
    nf_ref[...] = n_sc[...]
    mf_ref[...] = m_sc[...]


def _mlstm_scan(z, gates, gates_t, c0, n0, m0, *, row0, nseq, seqlen, reverse):
    cb = min(256, seqlen)
    n_sub = cb // A_CHUNK
    nblk = seqlen // cb
    blk0 = row0 // cb

    def rmap(b, c):
        cc = (nblk - 1 - c) if reverse else c
        return blk0 + b * nblk + cc

    def omap(b, c):
        cc = (nblk - 1 - c) if reverse else c
        return b * nblk + cc

    kern = functools.partial(_mlstm_scan_kernel, reverse=reverse, n_sub=n_sub)
    return pl.pallas_call(
        kern,
        grid=(nseq, nblk),
        in_specs=[
            pl.BlockSpec((cb, A_QK), lambda b, c: (rmap(b, c), 0)),
            pl.BlockSpec((cb, A_QK), lambda b, c: (rmap(b, c), 1)),
            pl.BlockSpec((cb, A_V), lambda b, c: (rmap(b, c), 1)),
            pl.BlockSpec((cb, A_GATE_COLS), lambda b, c: (rmap(b, c), 0)),
            pl.BlockSpec((4 * A_HEADS, cb), lambda b, c: (0, rmap(b, c))),
            pl.BlockSpec((None, A_HEADS, A_DK, A_DV), lambda b, c: (b, 0, 0, 0)),
            pl.BlockSpec((None, A_HEADS, A_DK), lambda b, c: (b, 0, 0)),
            pl.BlockSpec((None, A_HEADS, 1), lambda b, c: (b, 0, 0)),
        ],
        out_specs=[
            pl.BlockSpec((cb, A_V), lambda b, c: (omap(b, c), 0)),
            pl.BlockSpec((None, A_HEADS, A_DK, A_DV), lambda b, c: (b, 0, 0, 0)),
            pl.BlockSpec((None, A_HEADS, A_DK), lambda b, c: (b, 0, 0)),
            pl.BlockSpec((None, A_HEADS, 1), lambda b, c: (b, 0, 0)),
        ],
        out_shape=[
            jax.ShapeDtypeStruct((nseq * seqlen, A_V), F32),
            jax.ShapeDtypeStruct((nseq, A_HEADS, A_DK, A_DV), F32),
            jax.ShapeDtypeStruct((nseq, A_HEADS, A_DK), F32),
            jax.ShapeDtypeStruct((nseq, A_HEADS, 1), F32),
        ],
        scratch_shapes=[
            pltpu.VMEM((A_HEADS, A_DK, A_DV), F32),
            pltpu.VMEM((A_HEADS, A_DK), F32),
            pltpu.VMEM((A_HEADS, 1), F32),
        ],
        compiler_params=_cparams("parallel", "arbitrary"),
        name="mlstm_scan_bwd" if reverse else "mlstm_scan_fwd",
    )(z, z, z, gates, gates_t, c0, n0, m0)


def _rwkv_scan_kernel(r_ref, k_ref, v_ref, lw_ref, a_ref, kk_ref, ka_ref, rk_ref, s0_ref,
                      y_ref, sf_ref, s_sc, *, reverse, n_sub, prec):
    L = B_CHUNK
    W = B_PAIR_W
    R = 2 * L

    @pl.when(pl.program_id(1) == 0)
    def _():
        s_sc[...] = s0_ref[...]

    lane = lax.broadcasted_iota(jnp.int32, (L, W), 1)
    head0 = lane < B_HEAD_DIM
    ri = lax.broadcasted_iota(jnp.int32, (R, R), 0)
    ci = lax.broadcasted_iota(jnp.int32, (R, R), 1)
    same_head = (ri >= L) == (ci >= L)
    strict = same_head & ((ri < ci) if reverse else (ri > ci))
    incl = same_head & ((ri <= ci) if reverse else (ri >= ci))
    eye = (ri == ci).astype(F32)
    wi = lax.broadcasted_iota(jnp.int32, (W, W), 0)
    wj = lax.broadcasted_iota(jnp.int32, (W, W), 1)
    bd = (wi >= B_HEAD_DIM) == (wj >= B_HEAD_DIM)
    bd_ones = bd.astype(F32)
    ti = lax.broadcasted_iota(jnp.int32, (L, L), 0)
    tj = lax.broadcasted_iota(jnp.int32, (L, L), 1)
    tri = ((tj >= ti) if reverse else (tj <= ti)).astype(F32)

    def stack_masked(x):
        return jnp.concatenate([jnp.where(head0, x, 0.0), jnp.where(head0, 0.0, x)], axis=0)

    def stack_dup(x):
        return jnp.concatenate([x, x], axis=0)

    subs = range(n_sub - 1, -1, -1) if reverse else range(n_sub)
    for sub in subs:
        r0 = sub * L
        for p in range(B_PAIRS):
            sl = slice(p * W, (p + 1) * W)
            r = r_ref[r0:r0 + L, sl]
            k = k_ref[r0:r0 + L, sl]
            v = v_ref[r0:r0 + L, sl]
            lw = -B_DECAY_SCALE * jax.nn.sigmoid(lw_ref[r0:r0 + L, sl])
            a = jax.nn.sigmoid(a_ref[r0:r0 + L, sl])
            kkr = k * kk_ref[:, sl]
            ss = _dot(kkr * kkr, bd_ones, HIGHEST)
            kk = kkr / jnp.maximum(jnp.sqrt(ss), 1e-12)
            kd = k * (1.0 + (a - 1.0) * ka_ref[:, sl])
            bv = kk * a
            bonus = _dot(r * kd * rk_ref[:, sl], bd_ones, HIGHEST) * v

            c_inc = _dot(tri, lw, HIGHEST)
            c_exc = c_inc - lw
            c_mid = c_inc[L // 2:L // 2 + 1, :]
            tot = c_inc[0:1, :] if reverse else c_inc[L - 1:L, :]
            e_mid = jnp.exp(-c_mid)
            xa_h = stack_masked(-kk * jnp.exp(c_exc))
            xr_h = stack_masked(r * jnp.exp(c_inc))
            e_neg = jnp.exp(c_mid - c_inc)
            e_end = jnp.exp(tot - c_inc)
            x_all = jnp.concatenate([xa_h, xr_h], axis=0)
            y_all = jnp.concatenate([stack_dup(bv * e_neg), stack_dup(kd * e_neg)], axis=0)
            y_end = jnp.concatenate([stack_dup(bv * e_end), stack_dup(kd * e_end)], axis=0)
            v_s = stack_masked(v)

            a_all = _dot_nt(x_all * e_mid, y_all, prec)
            a_ab = jnp.where(strict, a_all[:R, :R], 0.0)
            a_ak = jnp.where(strict, a_all[:R, R:], 0.0)
            a_rb = jnp.where(incl, a_all[R:, :R], 0.0)
            a_rk = jnp.where(incl, a_all[R:, R:], 0.0)

            t_inv = eye + a_ab
            pw = a_ab
            for _ in range(5):
                pw = _dot(pw, pw, prec)
                t_inv = t_inv + _dot(pw, t_inv, prec)

            s_prev = s_sc[p]
            xs = _dot_nt(x_all, s_prev, prec)
            u_s = _dot(t_inv, xs[:R] + _dot(a_ak, v_s, prec), prec)
            uv = jnp.concatenate([u_s, v_s], axis=0)
            y_s = xs[R:] + _dot(jnp.concatenate([a_rb, a_rk], axis=1), uv, prec)
            y_ref[r0:r0 + L, sl] = y_s[:L] + y_s[L:] + bonus
            s_new = s_prev * jnp.exp(tot) + _dot_tn(uv, y_end, prec)
            s_sc[p] = jnp.where(bd, s_new, 0.0)

    sf_ref[...] = s_sc[...]


def _rwkv_scan(r, k, v, lw, a, k_k, k_a, r_k, s0, *, row0, nseq, seqlen, reverse, prec):
    cb = min(128, seqlen)
    n_sub = cb // B_CHUNK
    nblk = seqlen // cb
    blk0 = row0 // cb

    def rmap(b, c):
        cc = (nblk - 1 - c) if reverse else c
        return (blk0 + b * nblk + cc, 0)

    def omap(b, c):
        cc = (nblk - 1 - c) if reverse else c
        return (b * nblk + cc, 0)

    tile = pl.BlockSpec((cb, D_MODEL), rmap)
    vec = pl.BlockSpec((1, D_MODEL), lambda b, c: (0, 0))
    st = pl.BlockSpec((None, B_PAIRS, B_PAIR_W, B_PAIR_W), lambda b, c: (b, 0, 0, 0))
    kern = functools.partial(_rwkv_scan_kernel, reverse=reverse, n_sub=n_sub, prec=prec)
    return pl.pallas_call(
        kern,
        grid=(nseq, nblk),
        in_specs=[tile, tile, tile, tile, tile, vec, vec, vec, st],
        out_specs=[pl.BlockSpec((cb, D_MODEL), omap), st],
        out_shape=[
            jax.ShapeDtypeStruct((nseq * seqlen, D_MODEL), F32),
            jax.ShapeDtypeStruct((nseq, B_PAIRS, B_PAIR_W, B_PAIR_W), F32),
        ],
        scratch_shapes=[pltpu.VMEM((B_PAIRS, B_PAIR_W, B_PAIR_W), F32)],
        compiler_params=_cparams("parallel", "arbitrary"),
        name="rwkv_scan_bwd" if reverse else "rwkv_scan_fwd",
    )(r, k, v, lw, a, k_k, k_a, r_k, s0)


def _pair_blockdiag(s):
    n = s.shape[0]
    s = s.reshape(n, B_PAIRS, 2, B_HEAD_DIM, B_HEAD_DIM)
    z = jnp.zeros_like(s[:, :, 0])
    top = jnp.concatenate([s[:, :, 0], z], axis=-1)
    bot = jnp.concatenate([z, s[:, :, 1]], axis=-1)
    return jnp.concatenate([top, bot], axis=-2)


def _pair_unblock(sp):
    n = sp.shape[0]
    h0 = sp[:, :, :B_HEAD_DIM, :B_HEAD_DIM]
    h1 = sp[:, :, B_HEAD_DIM:, B_HEAD_DIM:]
    return jnp.stack([h0, h1], axis=2).reshape(n, B_HEADS, B_HEAD_DIM, B_HEAD_DIM)


TM = 256
TM_MOE = 1024


def _mod_spec(layer, tm):
    per_group = GROUP_ROWS // tm
    return pl.BlockSpec((None, None, N_ADA, D_MODEL), lambda i, *_: (layer, i // per_group, 0, 0))


def _row_spec(tm, cols, col_block=0):
    return pl.BlockSpec((tm, cols), lambda i, *_: (i, col_block))


def _full_spec(shape):
    nd = len(shape)
    return pl.BlockSpec(shape, lambda i, *_: (0,) * nd)


def _in_proj_kernel(x_ref, mod_ref, g_ref, w_ref, b_ref, wg_ref, bg_ref, z_ref, gate_ref):
    hn = _norm_mod(x_ref[...], g_ref[...], mod_ref[1:2, :], mod_ref[0:1, :])
    z_ref[...] = _dot(hn.astype(BF16), w_ref[...]) + b_ref[...]
    gate_ref[...] = _dot(hn, wg_ref[...], HIGHEST) + bg_ref[...]


def _in_proj(x, mod, g1, w_main, b_main, w_gate, b_gate):
    rows = x.shape[0]
    cols = w_main.shape[1]
    return pl.pallas_call(
        _in_proj_kernel,
        grid=(rows // TM,),
        in_specs=[
            _row_spec(TM, D_MODEL), _mod_spec(0, TM), _full_spec((1, D_MODEL)),
            _full_spec(w_main.shape), _full_spec(b_main.shape), _full_spec(w_gate.shape), _full_spec(b_gate.shape),
        ],
        out_specs=[_row_spec(TM, cols), _row_spec(TM, A_GATE_COLS)],
        out_shape=[jax.ShapeDtypeStruct((rows, cols), F32), jax.ShapeDtypeStruct((rows, A_GATE_COLS), F32)],
        compiler_params=_cparams("parallel"),
        name="mlstm_in_proj",
    )(x, mod, g1, w_main, b_main, w_gate, b_gate)


def _mlstm_out_ff_kernel(hf_ref, hb_ref, o_ref, x_ref, mod_ref, modn_ref, gh_ref, wo_ref, g2_ref,
                         wgu_ref, wdn_ref, gn_ref, x_out_ref, hn_out_ref):
    hs = hf_ref[...] + hb_ref[...]
    parts = []
    for h in range(A_HEADS):
        seg = hs[:, h * A_DV:(h + 1) * A_DV]
        parts.append(seg * lax.rsqrt(jnp.mean(seg * seg, axis=-1, keepdims=True) + RMS_EPS))
    y = (jnp.concatenate(parts, axis=1) * gh_ref[...]) * jax.nn.sigmoid(o_ref[...])
    x1 = x_ref[...] + mod_ref[2:3, :] * _dot(y.astype(BF16), wo_ref[...])
    hn2 = _norm_mod(x1, g2_ref[...], mod_ref[4:5, :], mod_ref[3:4, :])
    gu = _dot(hn2.astype(BF16), wgu_ref[...])
    act = jax.nn.silu(gu[:, :D_FF]) * gu[:, D_FF:]
    x2 = x1 + mod_ref[5:6, :] * _dot(act.astype(BF16), wdn_ref[...])
    x_out_ref[...] = x2
    hn_out_ref[...] = _norm_mod(x2, gn_ref[...], modn_ref[1:2, :], modn_ref[0:1, :])


def _mlstm_out_ff(hf, hb, z, x, mod, g_head, w_out, g2, w_gu, w_down, g1_next):
    rows = x.shape[0]
    return pl.pallas_call(
        _mlstm_out_ff_kernel,
        grid=(rows // TM,),
        in_specs=[
            _row_spec(TM, A_V), _row_spec(TM, A_V), _row_spec(TM, A_V, 2), _row_spec(TM, D_MODEL),
            _mod_spec(0, TM), _mod_spec(1, TM), _full_spec((1, A_V)), _full_spec(w_out.shape),
            _full_spec((1, D_MODEL)), _full_spec(w_gu.shape), _full_spec(w_down.shape), _full_spec((1, D_MODEL)),
        ],
        out_specs=[_row_spec(TM, D_MODEL), _row_spec(TM, D_MODEL)],
        out_shape=[jax.ShapeDtypeStruct((rows, D_MODEL), F32)] * 2,
        compiler_params=_cparams("parallel"),
        name="mlstm_out_ff",
    )(hf, hb, z, x, mod, mod, g_head, w_out, g2, w_gu, w_down, g1_next)


def _rwkv_in_kernel(hn_ref, hs_ref, mu_ref, wrkv_ref, w1_ref, w2_ref, w0_ref, a1_ref, a2_ref, a0_ref,
                    g1_ref, g2_ref, r_ref, k_ref, v_ref, lw0_ref, lw1_ref, ia0_ref, ia1_ref, g_ref):
    h = hn_ref[...]
    xx = hs_ref[...] - h

    def mix(c):
        return (h + xx * mu_ref[c:c + 1, :]).astype(BF16)

    r_ref[...] = _dot(mix(0), wrkv_ref[0])
    k_ref[...] = _dot(mix(1), wrkv_ref[1])
    v_ref[...] = _dot(mix(2), wrkv_ref[2])
    lane = lax.broadcasted_iota(jnp.int32, (h.shape[0], 2 * 64), 1)
    dir0 = lane < 64
    t = jnp.tanh(_dot(mix(3), w1_ref[...]))
    lw0_ref[...] = w0_ref[0:1, :] + _dot(jnp.where(dir0, t, 0.0).astype(BF16), w2_ref[...])
    lw1_ref[...] = w0_ref[1:2, :] + _dot(jnp.where(dir0, 0.0, t).astype(BF16), w2_ref[...])
    u = _dot(mix(4), a1_ref[...])
    ia0_ref[...] = a0_ref[0:1, :] + _dot(jnp.where(dir0, u, 0.0).astype(BF16), a2_ref[...])
    ia1_ref[...] = a0_ref[1:2, :] + _dot(jnp.where(dir0, 0.0, u).astype(BF16), a2_ref[...])
    g_ref[...] = _dot(jax.nn.sigmoid(_dot(mix(5), g1_ref[...])).astype(BF16), g2_ref[...])


def _rwkv_in(hn, hshift, mu, w_rkv, w1, w2, w0, a1, a2, a0, g1, g2):
    rows = hn.shape[0]
    ins = [mu, w_rkv, w1, w2, w0, a1, a2, a0, g1, g2]
    return pl.pallas_call(
        _rwkv_in_kernel,
        grid=(rows // TM,),
        in_specs=[_row_spec(TM, D_MODEL), _row_spec(TM, D_MODEL)] + [_full_spec(a.shape) for a in ins],
        out_specs=[_row_spec(TM, D_MODEL)] * 8,
        out_shape=[jax.ShapeDtypeStruct((rows, D_MODEL), F32)] * 8,
        compiler_params=_cparams("parallel"),
        name="rwkv_in_proj",
    )(hn, hshift, *ins)


def _rwkv_out_kernel(yf_ref, yb_ref, g_ref, x_ref, mod_ref, lng_ref, lnb_ref, wo_ref, x_out_ref):
    y = yf_ref[...] + yb_ref[...]
    wi = lax.broadcasted_iota(jnp.int32, (B_PAIR_W, B_PAIR_W), 0)
    wj = lax.broadcasted_iota(jnp.int32, (B_PAIR_W, B_PAIR_W), 1)
    bd_mean = jnp.where((wi >= B_HEAD_DIM) == (wj >= B_HEAD_DIM), 1.0 / B_HEAD_DIM, 0.0)
    parts = []
    for p in range(B_PAIRS):
        seg = y[:, p * B_PAIR_W:(p + 1) * B_PAIR_W]
        cen = seg - _dot(seg, bd_mean, HIGHEST)
        var = _dot(cen * cen, bd_mean, HIGHEST)
        parts.append(cen * lax.rsqrt(var + B_LN_EPS))
    yn = jnp.concatenate(parts, axis=1) * lng_ref[...] + lnb_ref[...]
    out = _dot((yn * g_ref[...]).astype(BF16), wo_ref[...])
    x_out_ref[...] = x_ref[...] + mod_ref[2:3, :] * out


def _rwkv_out(yf, yb, g, x, mod, ln_g, ln_b, w_out):
    rows = x.shape[0]
    return pl.pallas_call(
        _rwkv_out_kernel,
        grid=(rows // TM,),
        in_specs=[_row_spec(TM, D_MODEL)] * 4 + [
            _mod_spec(1, TM), _full_spec((1, D_MODEL)), _full_spec((1, D_MODEL)), _full_spec(w_out.shape)],
        out_specs=_row_spec(TM, D_MODEL),
        out_shape=jax.ShapeDtypeStruct((rows, D_MODEL), F32),
        compiler_params=_cparams("parallel"),
        name="rwkv_out_proj",
    )(yf, yb, g, x, mod, ln_g, ln_b, w_out)


def _moe_kernel(x_ref, mod_ref, g2_ref, wr_ref, br_ref, wgu_ref, wdn_ref, gf_ref, out_ref,
                hn_sc, comb_sc, acc_sc):
    e = pl.program_id(1)
    lane = lax.broadcasted_iota(jnp.int32, comb_sc.shape, 1)

    @pl.when(e == 0)
    def _():
        hn = _norm_mod(x_ref[...], g2_ref[...], mod_ref[4:5, :], mod_ref[3:4, :])
        hn_sc[...] = hn.astype(BF16)
        logits = _dot(hn, wr_ref[...], HIGHEST) + br_ref[...]
        m1 = jnp.max(logits, axis=1, keepdims=True)
        i1 = jnp.min(jnp.where(logits == m1, lane, ROUTER_COLS), axis=1, keepdims=True)
        rest = jnp.where(lane == i1, -jnp.inf, logits)
        m2 = jnp.max(rest, axis=1, keepdims=True)
        i2 = jnp.min(jnp.where(rest == m2, lane, ROUTER_COLS), axis=1, keepdims=True)
        e2 = jnp.exp(m2 - m1)
        den = 1.0 + e2
        comb_sc[...] = jnp.where(lane == i1, 1.0 / den, 0.0) + jnp.where(lane == i2, e2 / den, 0.0)
        acc_sc[...] = jnp.zeros_like(acc_sc)

    gu = _dot(hn_sc[...], wgu_ref[...])
    act = jax.nn.silu(gu[:, :D_FF_EXPERT]) * gu[:, D_FF_EXPERT:]
    ye = _dot(act.astype(BF16), wdn_ref[...])
    ce = jnp.sum(jnp.where(lane == e, comb_sc[...], 0.0), axis=1, keepdims=True)
    acc_sc[...] += ce * ye

    @pl.when(e == N_EXPERTS - 1)
    def _():
        x4 = x_ref[...] + mod_ref[5:6, :] * acc_sc[...]
        out_ref[...] = (x4 * lax.rsqrt(jnp.mean(x4 * x4, axis=-1, keepdims=True) + RMS_EPS)) * gf_ref[...]


def _moe(x, mod, g2, w_router, b_router, w_gu, w_down, g_final):
    rows = x.shape[0]
    tm = TM_MOE
    return pl.pallas_call(
        _moe_kernel,
        grid=(rows // tm, N_EXPERTS),
        in_specs=[
            _row_spec(tm, D_MODEL), _mod_spec(1, tm), _full_spec((1, D_MODEL)),
            _full_spec(w_router.shape), _full_spec(b_router.shape),
            pl.BlockSpec((None, D_MODEL, 2 * D_FF_EXPERT), lambda i, e: (e, 0, 0)),
            pl.BlockSpec((None, D_FF_EXPERT, D_MODEL), lambda i, e: (e, 0, 0)),
            _full_spec((1, D_MODEL)),
        ],
        out_specs=_row_spec(tm, D_MODEL),
        out_shape=jax.ShapeDtypeStruct((rows, D_MODEL), F32),
        scratch_shapes=[
            pltpu.VMEM((tm, D_MODEL), BF16),
            pltpu.VMEM((tm, ROUTER_COLS), F32),
            pltpu.VMEM((tm, D_MODEL), F32),
        ],
        compiler_params=_cparams("parallel", "arbitrary"),
        name="moe_final",
    )(x, mod, g2, w_router, b_router, w_gu, w_down, g_final)


def _shift_1d(x):
    half = x.shape[-1] // 2
    prev = jnp.pad(x[:, :-1, :half], ((0, 0), (1, 0), (0, 0)))
    nxt = jnp.pad(x[:, 1:, half:], ((0, 0), (0, 1), (0, 0)))
    return jnp.concatenate([prev, nxt], axis=-1)


def _shift_grid(x):
    b, t, d = x.shape
    g = x.reshape(b, t // GRID_W, GRID_W, d)
    q = d // 4
    left = jnp.pad(g[:, :, :-1, :q], ((0, 0), (0, 0), (1, 0), (0, 0)))
    right = jnp.pad(g[:, :, 1:, q:2 * q], ((0, 0), (0, 0), (0, 1), (0, 0)))
    up = jnp.pad(g[:, :-1, :, 2 * q:3 * q], ((0, 0), (1, 0), (0, 0), (0, 0)))
    down = jnp.pad(g[:, 1:, :, 3 * q:], ((0, 0), (0, 1), (0, 0), (0, 0)))
    return jnp.concatenate([left, right, up, down], axis=-1).reshape(b, t, d)


def _pad_cols(a, cols, value=0.0):
    return jnp.pad(a, ((0, 0), (0, cols - a.shape[1])), constant_values=value)


def kernel(x_prompt, x_sample, state_mlstm_C, state_mlstm_n, state_mlstm_m, state_rwkv_S, c, c_ctx, w_ada, b_ada, g_norm1, g_norm2, w_m_in, b_m_in, g_m_head, w_m_out, mu_rwkv, w_rkv, w_decay0, w_decay1, w_decay2, w_iclr0, w_iclr1, w_iclr2, w_gate1, w_gate2, k_k, k_a, r_k, g_ln_x, b_ln_x, w_rwkv_out, w_ff_gu, w_ff_down, w_router, b_router, w_exp_gu, w_exp_down, g_final):
    nb, seq, d = x_prompt.shape
    db, dseq, _ = x_sample.shape
    n_ctx = nb * seq
    n_lat = db * dseq
    assert d == D_MODEL and n_ctx == GROUP_ROWS and dseq == GROUP_ROWS

    x = jnp.concatenate([x_prompt.reshape(n_ctx, d), x_sample.reshape(n_lat, d)], axis=0)
    cond = jnp.concatenate([c_ctx[None, :], c], axis=0)
    n_groups = cond.shape[0]
    cond = jnp.pad(cond, ((0, (-n_groups) % 8), (0, 0)))
    mod = _ada(cond, w_ada, b_ada).reshape(w_ada.shape[0], cond.shape[0], N_ADA, d)
    row = lambda a: a.reshape(1, -1)

    n_main = 2 * A_QK + 2 * A_V
    w_in = w_m_in[0]
    z, gates = _in_proj(x, mod, row(g_norm1[0]), w_in[:, :n_main].astype(BF16), row(b_m_in[0][:n_main]),
                        _pad_cols(w_in[:, n_main:], A_GATE_COLS), _pad_cols(row(b_m_in[0][n_main:]), A_GATE_COLS))
    gates_t = gates[:, :4 * A_HEADS].T
    zc = jnp.zeros((nb, A_HEADS, A_DK, A_DV), F32)
    zn = jnp.zeros((nb, A_HEADS, A_DK), F32)
    zm = jnp.zeros((nb, A_HEADS, 1), F32)
    h_dirs, ctx_state = [], []
    for dirn, rev in ((0, False), (1, True)):
        h_c, cf, nf, mf = _mlstm_scan(z, gates, gates_t, zc, zn, zm, row0=0, nseq=nb, seqlen=seq, reverse=rev)
        h_l, _, _, _ = _mlstm_scan(z, gates, gates_t, state_mlstm_C[:, 0, dirn], state_mlstm_n[:, 0, dirn],
                                   state_mlstm_m[:, 0, dirn][..., None], row0=n_ctx, nseq=db, seqlen=dseq, reverse=rev)
        h_dirs.append(jnp.concatenate([h_c, h_l], axis=0))
        ctx_state.append((cf, nf, mf[..., 0]))
    new_c = jnp.stack([s[0] for s in ctx_state], axis=1)[:, None]
    new_n = jnp.stack([s[1] for s in ctx_state], axis=1)[:, None]
    new_m = jnp.stack([s[2] for s in ctx_state], axis=1)[:, None]
    x, hn = _mlstm_out_ff(h_dirs[0], h_dirs[1], z, x, mod, row(g_m_head[0]), w_m_out[0].astype(BF16),
                          row(g_norm2[0]), w_ff_gu[0].astype(BF16), w_ff_down[0].astype(BF16), row(g_norm1[1]))

    hshift = jnp.concatenate([
        _shift_1d(hn[:n_ctx].reshape(nb, seq, d)).reshape(n_ctx, d),
        _shift_grid(hn[n_ctx:].reshape(db, dseq, d)).reshape(n_lat, d)], axis=0)
    cat_cols = lambda w: jnp.concatenate([w[0], w[1]], axis=1).astype(BF16)
    cat_rows = lambda w: jnp.concatenate([w[0], w[1]], axis=0).astype(BF16)
    r, k, v, lw0, lw1, ia0, ia1, g = _rwkv_in(
        hn, hshift, mu_rwkv[0], w_rkv[0].astype(BF16), cat_cols(w_decay1[0]), cat_rows(w_decay2[0]), w_decay0[0],
        cat_cols(w_iclr1[0]), cat_rows(w_iclr2[0]), w_iclr0[0], w_gate1[0].astype(BF16), w_gate2[0].astype(BF16))
    zs = jnp.zeros((nb, B_PAIRS, B_PAIR_W, B_PAIR_W), F32)
    y_dirs, ctx_s = [], []
    for dirn, rev, lw, ia in ((0, False, lw0, ia0), (1, True, lw1, ia1)):
        args = (r, k, v, lw, ia, row(k_k[0]), row(k_a[0]), row(r_k[0]))
        y_c, sf = _rwkv_scan(*args, zs, row0=0, nseq=nb, seqlen=seq, reverse=rev, prec=HIGHEST)
        y_l, _ = _rwkv_scan(*args, _pair_blockdiag(state_rwkv_S[:, 0, dirn]), row0=n_ctx, nseq=db, seqlen=dseq,
                            reverse=rev, prec=HIGHEST)
        y_dirs.append(jnp.concatenate([y_c, y_l], axis=0))
        ctx_s.append(_pair_unblock(sf))
    new_s = jnp.stack(ctx_s, axis=1)[:, None]
    x = _rwkv_out(y_dirs[0], y_dirs[1], g, x, mod, row(g_ln_x[0]), row(b_ln_x[0]), w_rwkv_out[0].astype(BF16))
    y = _moe(x, mod, row(g_norm2[1]), _pad_cols(w_router[0], ROUTER_COLS),
             _pad_cols(row(b_router[0]), ROUTER_COLS, NEG_BIG), w_exp_gu[0].astype(BF16), w_exp_down[0].astype(BF16),
             row(g_final))
    return (y[:n_ctx].reshape(nb, seq, d), y[n_ctx:].reshape(db, dseq, d), new_c, new_n, new_m, new_s)
```

```python
import functools

import jax
import jax.numpy as jnp
from jax import lax
from jax.experimental import pallas as pl
from jax.experimental.pallas import tpu as pltpu

F32 = jnp.float32
BF16 = jnp.bfloat16
HIGHEST = lax.Precision.HIGHEST

D_MODEL = 1024
GROUP_ROWS = 4096
N_ADA = 6
RMS_EPS = 1e-6
GRID_W = 64
A_HEADS = 8
A_DK = 64
A_DV = 128
A_QK = A_HEADS * A_DK
A_V = A_HEADS * A_DV
A_CHUNK = 128
A_GATE_COLS = 128
B_HEADS = 16
B_HEAD_DIM = 64
B_PAIRS = B_HEADS // 2
B_PAIR_W = 2 * B_HEAD_DIM
B_CHUNK = 64
B_DECAY_SCALE = 0.606531
B_LN_EPS = 64e-5
B_SCAN_PASSES = (1, 1, 1, 1)
D_FF = 2816
N_EXPERTS = 8
D_FF_EXPERT = D_FF // 2
ROUTER_COLS = 128
NEG_BIG = -1e30

VMEM_LIMIT = 56 * 1024 * 1024


def _cparams(*sem):
    return pltpu.CompilerParams(dimension_semantics=sem, vmem_limit_bytes=VMEM_LIMIT)


def _dot(a, b, precision=None):
    return jnp.dot(a, b, precision=precision, preferred_element_type=F32)


def _dot_nt(a, b, precision=None):
    return lax.dot_general(a, b, (((1,), (1,)), ((), ())), precision=precision, preferred_element_type=F32)


def _dot_tn(a, b, precision=None):
    return lax.dot_general(a, b, (((0,), (0,)), ((), ())), precision=precision, preferred_element_type=F32)


_NN = (((1,), (0,)), ((), ()))
_NT = (((1,), (1,)), ((), ()))
_TN = (((0,), (0,)), ((), ()))


def _bf16_terms(a, n):
    terms = []
    for i in range(n):
        t = a.astype(BF16)
        terms.append(t)
        if i + 1 < n:
            a = a - t.astype(F32)
    return terms


def _mxu(a, b, dims=_NN, na=1, nb=1):
    at = _bf16_terms(a, na)
    bt = _bf16_terms(b, nb)
    out = None
    for i, x in enumerate(at):
        for j, y in enumerate(bt):
            if i + j < max(na, nb):
                t = lax.dot_general(x, y, dims, preferred_element_type=F32)
                out = t if out is None else out + t
    return out


def _norm_mod(x, g, sc, sh):
    y = x * lax.rsqrt(jnp.mean(x * x, axis=-1, keepdims=True) + RMS_EPS)
    return (y * g) * (1.0 + sc) + sh


def _log_sigmoid(x):
    return jnp.minimum(x, 0.0) - jnp.log1p(jnp.exp(-jnp.abs(x)))


def _ada_kernel(cond_ref, w_ref, b_ref, out_ref):
    c = cond_ref[...]
    out_ref[...] = _dot(c * jax.nn.sigmoid(c), w_ref[...], HIGHEST) + b_ref[...]


def _ada(cond_pad, w_ada, b_ada):
    depth, d, cols = w_ada.shape
    rows = cond_pad.shape[0]
    tn = 1536
    return pl.pallas_call(
        _ada_kernel,
        grid=(depth, cols // tn),
        in_specs=[
            pl.BlockSpec((rows, d), lambda l, j: (0, 0)),
            pl.BlockSpec((None, d, tn), lambda l, j: (l, 0, j)),
            pl.BlockSpec((None, 1, tn), lambda l, j: (l, 0, j)),
        ],
        out_specs=pl.BlockSpec((None, rows, tn), lambda l, j: (l, 0, j)),
        out_shape=jax.ShapeDtypeStruct((depth, rows, cols), F32),
        compiler_params=_cparams("parallel", "parallel"),
        name="ada",
    )(cond_pad, w_ada, b_ada.reshape(depth, 1, cols))


def _mlstm_scan_kernel(q_ref, k_ref, v_ref, gc_ref, gr_ref, c0_ref, n0_ref, m0_ref,
                       h_ref, cf_ref, nf_ref, mf_ref, c_sc, n_sc, m_sc, *, reverse, n_sub):
    L = A_CHUNK

    @pl.when(pl.program_id(1) == 0)
    def _():
        c_sc[...] = c0_ref[...]
        n_sc[...] = n0_ref[...]
        m_sc[...] = m0_ref[...]

    row = lax.broadcasted_iota(jnp.int32, (L, L), 0)
    col = lax.broadcasted_iota(jnp.int32, (L, L), 1)
    in_win = (col >= row) if reverse else (col <= row)
    in_win_t = (row >= col) if reverse else (row <= col)
    d0 = A_HEADS if reverse else 0

    subs = range(n_sub - 1, -1, -1) if reverse else range(n_sub)
    for sub in subs:
        r0 = sub * L
        gcol = gc_ref[r0:r0 + L, :]
        grow = gr_ref[:, r0:r0 + L]
        fcol_all = _log_sigmoid(gcol)
        frow_all = _log_sigmoid(grow)
        for h in range(A_HEADS):
            ci = d0 + h
            cf = 2 * A_HEADS + d0 + h
            i_col = gcol[:, ci:ci + 1]
            f_col = fcol_all[:, cf:cf + 1]
            i_row = grow[ci:ci + 1, :]
            f_row = frow_all[cf:cf + 1, :]
            b_col = jnp.sum(jnp.where(in_win, f_row, 0.0), axis=1, keepdims=True)
            b_row = jnp.sum(jnp.where(in_win_t, f_col, 0.0), axis=0, keepdims=True)
            tot = jnp.sum(f_row, axis=1, keepdims=True)
            m_prev = m_sc[h:h + 1, :]
            logw = jnp.where(in_win, b_col - b_row + i_row, -jnp.inf)
            inter = b_col + m_prev
            m_row = jnp.maximum(inter, jnp.max(logw, axis=1, keepdims=True))
            w_intra = jnp.exp(logw - m_row)
            w_inter = jnp.exp(inter - m_row)
            qh = q_ref[r0:r0 + L, h * A_DK:(h + 1) * A_DK]
            kh = k_ref[r0:r0 + L, h * A_DK:(h + 1) * A_DK] * (A_DK ** -0.5)
            vh = v_ref[r0:r0 + L, h * A_DV:(h + 1) * A_DV]
            c_prev = c_sc[h]
            n_prev = n_sc[h:h + 1, :]
            s = _dot_nt(qh, kh) * w_intra
            num = _dot(s, vh) + w_inter * _dot(qh, c_prev)
            den = jnp.sum(s, axis=1, keepdims=True) + w_inter * jnp.sum(qh * n_prev, axis=1, keepdims=True)
            h_ref[r0:r0 + L, h * A_DV:(h + 1) * A_DV] = num / jnp.maximum(jnp.abs(den), jnp.exp(-m_row))
            log_g = tot - b_col + i_col
            m_new = jnp.maximum(tot + m_prev, jnp.max(log_g, axis=0, keepdims=True))
            decay = jnp.exp(tot + m_prev - m_new)
            kw = kh * jnp.exp(log_g - m_new)
            c_sc[h] = decay * c_prev + _dot_tn(kw, vh)
            n_sc[h:h + 1, :] = decay * n_prev + jnp.sum(kw, axis=0, keepdims=True)
            m_sc[h:h + 1, :] = m_new

    cf_ref[...] = c_sc[...]
    nf_ref[...] = n_sc[...]
    mf_ref[...] = m_sc[...]


def _mlstm_scan(z, gates, gates_t, c0, n0, m0, *, row0, nseq, seqlen, reverse):
    cb = min(256, seqlen)
    n_sub = cb // A_CHUNK
    nblk = seqlen // cb
    blk0 = row0 // cb

    def rmap(b, c):
        cc = (nblk - 1 - c) if reverse else c
        return blk0 + b * nblk + cc

    def omap(b, c):
        cc = (nblk - 1 - c) if reverse else c
        return b * nblk + cc

    kern = functools.partial(_mlstm_scan_kernel, reverse=reverse, n_sub=n_sub)
    return pl.pallas_call(
        kern,
        grid=(nseq, nblk),
        in_specs=[
            pl.BlockSpec((cb, A_QK), lambda b, c: (rmap(b, c), 0)),
            pl.BlockSpec((cb, A_QK), lambda b, c: (rmap(b, c), 1)),
            pl.BlockSpec((cb, A_V), lambda b, c: (rmap(b, c), 1)),
            pl.BlockSpec((cb, A_GATE_COLS), lambda b, c: (rmap(b, c), 0)),
            pl.BlockSpec((4 * A_HEADS, cb), lambda b, c: (0, rmap(b, c))),
            pl.BlockSpec((None, A_HEADS, A_DK, A_DV), lambda b, c: (b, 0, 0, 0)),
            pl.BlockSpec((None, A_HEADS, A_DK), lambda b, c: (b, 0, 0)),
            pl.BlockSpec((None, A_HEADS, 1), lambda b, c: (b, 0, 0)),
        ],
        out_specs=[
            pl.BlockSpec((cb, A_V), lambda b, c: (omap(b, c), 0)),
            pl.BlockSpec((None, A_HEADS, A_DK, A_DV), lambda b, c: (b, 0, 0, 0)),
            pl.BlockSpec((None, A_HEADS, A_DK), lambda b, c: (b, 0, 0)),
            pl.BlockSpec((None, A_HEADS, 1), lambda b, c: (b, 0, 0)),
        ],
        out_shape=[
            jax.ShapeDtypeStruct((nseq * seqlen, A_V), F32),
            jax.ShapeDtypeStruct((nseq, A_HEADS, A_DK, A_DV), F32),
            jax.ShapeDtypeStruct((nseq, A_HEADS, A_DK), F32),
            jax.ShapeDtypeStruct((nseq, A_HEADS, 1), F32),
        ],
        scratch_shapes=[
            pltpu.VMEM((A_HEADS, A_DK, A_DV), F32),
            pltpu.VMEM((A_HEADS, A_DK), F32),
            pltpu.VMEM((A_HEADS, 1), F32),
        ],
        compiler_params=_cparams("parallel", "arbitrary"),
        name="mlstm_scan_bwd" if reverse else "mlstm_scan_fwd",
    )(z, z, z, gates, gates_t, c0, n0, m0)


def _rwkv_scan_kernel(r_ref, k_ref, v_ref, lw_ref, a_ref, kk_ref, ka_ref, rk_ref, s0_ref,
                      y_ref, sf_ref, s_sc, *, reverse, n_sub, passes):
    pa_a, pa_inv, pa_state, pa_out = passes
    L = B_CHUNK
    W = B_PAIR_W
    R = 2 * L

    @pl.when(pl.program_id(1) == 0)
    def _():
        s_sc[...] = s0_ref[...]

    lane = lax.broadcasted_iota(jnp.int32, (L, W), 1)
    head0 = lane < B_HEAD_DIM
    ri = lax.broadcasted_iota(jnp.int32, (R, R), 0)
    ci = lax.broadcasted_iota(jnp.int32, (R, R), 1)
    same_head = (ri >= L) == (ci >= L)
    strict = same_head & ((ri < ci) if reverse else (ri > ci))
    incl = same_head & ((ri <= ci) if reverse else (ri >= ci))
    eye = (ri == ci).astype(F32)
    wi = lax.broadcasted_iota(jnp.int32, (W, W), 0)
    wj = lax.broadcasted_iota(jnp.int32, (W, W), 1)
    bd = (wi >= B_HEAD_DIM) == (wj >= B_HEAD_DIM)
    bd_ones = jnp.where(bd, 1.0, 0.0).astype(BF16)
    ti = lax.broadcasted_iota(jnp.int32, (L, L), 0)
    tj = lax.broadcasted_iota(jnp.int32, (L, L), 1)
    tri = jnp.where((tj >= ti) if reverse else (tj <= ti), 1.0, 0.0).astype(BF16)

    def stack_masked(x):
        return jnp.concatenate([jnp.where(head0, x, 0.0), jnp.where(head0, 0.0, x)], axis=0)

    def stack_dup(x):
        return jnp.concatenate([x, x], axis=0)

    subs = list(range(n_sub - 1, -1, -1) if reverse else range(n_sub))
    units = [(sub * L, slice(p * W, (p + 1) * W)) for sub in subs for p in range(B_PAIRS)]
    nu = len(units)

    def each(fn, *lists):
        return [fn(*args) for args in zip(*lists)]

    r_l = [r_ref[r0:r0 + L, sl] for r0, sl in units]
    k_l = [k_ref[r0:r0 + L, sl] for r0, sl in units]
    v_l = [v_ref[r0:r0 + L, sl] for r0, sl in units]
    lw_l = [-B_DECAY_SCALE * jax.nn.sigmoid(lw_ref[r0:r0 + L, sl]) for r0, sl in units]
    a_l = [jax.nn.sigmoid(a_ref[r0:r0 + L, sl]) for r0, sl in units]
    kkr_l = [k * kk_ref[:, sl] for k, (_, sl) in zip(k_l, units)]
    kd_l = [k * (1.0 + (a - 1.0) * ka_ref[:, sl]) for k, a, (_, sl) in zip(k_l, a_l, units)]
    ss_l = [_mxu(kkr * kkr, bd_ones, na=3) for kkr in kkr_l]
    bsum_l = [_mxu(r * kd * rk_ref[:, sl], bd_ones, na=3) for r, kd, (_, sl) in zip(r_l, kd_l, units)]
    c_inc_l = [_mxu(tri, lw, nb=3) for lw in lw_l]
    kk_l = each(lambda kkr, ss: kkr / jnp.maximum(jnp.sqrt(ss), 1e-12), kkr_l, ss_l)
    bv_l = each(lambda kk, a: kk * a, kk_l, a_l)
    bonus_l = each(lambda bs, v: bs * v, bsum_l, v_l)
    c_mid_l = [c[L // 2:L // 2 + 1, :] for c in c_inc_l]
    tot_l = [(c[0:1, :] if reverse else c[L - 1:L, :]) for c in c_inc_l]
    x_all_l = each(lambda kk, r, c, lw: jnp.concatenate(
        [stack_masked(-kk * jnp.exp(c - lw)), stack_masked(r * jnp.exp(c))], axis=0), kk_l, r_l, c_inc_l, lw_l)
    xa_l = each(lambda x, cm: (x * jnp.exp(-cm)).astype(BF16), x_all_l, c_mid_l)
    x_all_l = [x.astype(BF16) for x in x_all_l]

    def decayed(bv, kd, e):
        return jnp.concatenate([stack_dup(bv * e), stack_dup(kd * e)], axis=0).astype(BF16)

    y_all_l = each(lambda bv, kd, c, cm: decayed(bv, kd, jnp.exp(cm - c)), bv_l, kd_l, c_inc_l, c_mid_l)
    y_end_l = each(lambda bv, kd, c, t: decayed(bv, kd, jnp.exp(t - c)), bv_l, kd_l, c_inc_l, tot_l)
    v_s_l = [stack_masked(v).astype(BF16) for v in v_l]

    a_all_l = each(lambda x, y: _mxu(x, y, _NT, pa_a, pa_a), xa_l, y_all_l)
    a_ab_l = [jnp.where(strict, a[:R, :R], 0.0) for a in a_all_l]
    a_ak_l = [jnp.where(strict, a[:R, R:], 0.0).astype(BF16) for a in a_all_l]
    a_r_l = [jnp.concatenate([jnp.where(incl, a[R:, :R], 0.0), jnp.where(incl, a[R:, R:], 0.0)],
                             axis=1).astype(BF16) for a in a_all_l]
    akv_l = each(lambda a, v: _mxu(a, v, _NN, pa_out, pa_out), a_ak_l, v_s_l)

    t_inv_l = [eye + a for a in a_ab_l]
    pw_l = [_mxu(a, a, _NN, pa_inv, pa_inv) for a in a_ab_l]
    for _ in range(4):
        both_l = each(lambda pw, t: _mxu(pw, jnp.concatenate([pw, t], axis=1), _NN, pa_inv, pa_inv), pw_l, t_inv_l)
        pw_l = [b[:, :R] for b in both_l]
        t_inv_l = each(lambda t, b: t + b[:, R:], t_inv_l, both_l)
    t_inv_l = each(lambda t, pw: (t + _mxu(pw, t, _NN, pa_inv, pa_inv)).astype(BF16), t_inv_l, pw_l)

    for i0 in range(0, nu, B_PAIRS):
        idx = list(range(i0, i0 + B_PAIRS))
        s_prev_l = [s_sc[p] for p in range(B_PAIRS)]
        xs_l = [_mxu(x_all_l[i], s, _NT, pa_state, pa_state) for i, s in zip(idx, s_prev_l)]
        u_l = [_mxu(t_inv_l[i], xs[:R] + akv_l[i], _NN, pa_inv, pa_inv) for i, xs in zip(idx, xs_l)]
        uv_l = [jnp.concatenate([u.astype(BF16), v_s_l[i]], axis=0) for i, u in zip(idx, u_l)]
        y_s_l = [xs[R:] + _mxu(a_r_l[i], uv, _NN, pa_out, pa_out) for i, xs, uv in zip(idx, xs_l, uv_l)]
        for i, y_s in zip(idx, y_s_l):
            r0, sl = units[i]
            y_ref[r0:r0 + L, sl] = y_s[:L] + y_s[L:] + bonus_l[i]
        s_new_l = [s * jnp.exp(tot_l[i]) + _mxu(uv, y_end_l[i], _TN, pa_state, pa_state)
                   for i, s, uv in zip(idx, s_prev_l, uv_l)]
        for p, s_new in enumerate(s_new_l):
            s_sc[p] = jnp.where(bd, s_new, 0.0)

    sf_ref[...] = s_sc[...]


def _rwkv_scan(r, k, v, lw, a, k_k, k_a, r_k, s0, *, row0, nseq, seqlen, reverse, passes=B_SCAN_PASSES):
    cb = min(128, seqlen)
    n_sub = cb // B_CHUNK
    nblk = seqlen // cb
    blk0 = row0 // cb

    def rmap(b, c):
        cc = (nblk - 1 - c) if reverse else c
        return (blk0 + b * nblk + cc, 0)

    def omap(b, c):
        cc = (nblk - 1 - c) if reverse else c
        return (b * nblk + cc, 0)

    tile = pl.BlockSpec((cb, D_MODEL), rmap)
    vec = pl.BlockSpec((1, D_MODEL), lambda b, c: (0, 0))
    st = pl.BlockSpec((None, B_PAIRS, B_PAIR_W, B_PAIR_W), lambda b, c: (b, 0, 0, 0))
    kern = functools.partial(_rwkv_scan_kernel, reverse=reverse, n_sub=n_sub, passes=passes)
    return pl.pallas_call(
        kern,
        grid=(nseq, nblk),
        in_specs=[tile, tile, tile, tile, tile, vec, vec, vec, st],
        out_specs=[pl.BlockSpec((cb, D_MODEL), omap), st],
        out_shape=[
            jax.ShapeDtypeStruct((nseq * seqlen, D_MODEL), F32),
            jax.ShapeDtypeStruct((nseq, B_PAIRS, B_PAIR_W, B_PAIR_W), F32),
        ],
        scratch_shapes=[pltpu.VMEM((B_PAIRS, B_PAIR_W, B_PAIR_W), F32)],
        compiler_params=_cparams("parallel", "arbitrary"),
        name="rwkv_scan_bwd" if reverse else "rwkv_scan_fwd",
    )(r, k, v, lw, a, k_k, k_a, r_k, s0)


def _pair_blockdiag(s):
    n = s.shape[0]
    s = s.reshape(n, B_PAIRS, 2, B_HEAD_DIM, B_HEAD_DIM)
    z = jnp.zeros_like(s[:, :, 0])
    top = jnp.concatenate([s[:, :, 0], z], axis=-1)
    bot = jnp.concatenate([z, s[:, :, 1]], axis=-1)
    return jnp.concatenate([top, bot], axis=-2)


def _pair_unblock(sp):
    n = sp.shape[0]
    h0 = sp[:, :, :B_HEAD_DIM, :B_HEAD_DIM]
    h1 = sp[:, :, B_HEAD_DIM:, B_HEAD_DIM:]
    return jnp.stack([h0, h1], axis=2).reshape(n, B_HEADS, B_HEAD_DIM, B_HEAD_DIM)


TM = 256
TM_MOE = 1024


def _mod_spec(layer, tm):
    per_group = GROUP_ROWS // tm
    return pl.BlockSpec((None, None, N_ADA, D_MODEL), lambda i, *_: (layer, i // per_group, 0, 0))


def _row_spec(tm, cols, col_block=0):
    return pl.BlockSpec((tm, cols), lambda i, *_: (i, col_block))


def _full_spec(shape):
    nd = len(shape)
    return pl.BlockSpec(shape, lambda i, *_: (0,) * nd)


def _in_proj_kernel(x_ref, mod_ref, g_ref, w_ref, b_ref, wg_ref, bg_ref, z_ref, gate_ref):
    hn = _norm_mod(x_ref[...], g_ref[...], mod_ref[1:2, :], mod_ref[0:1, :])
    z_ref[...] = _dot(hn.astype(BF16), w_ref[...]) + b_ref[...]
    gate_ref[...] = _dot(hn, wg_ref[...], HIGHEST) + bg_ref[...]


def _in_proj(x, mod, g1, w_main, b_main, w_gate, b_gate):
    rows = x.shape[0]
    cols = w_main.shape[1]
    return pl.pallas_call(
        _in_proj_kernel,
        grid=(rows // TM,),
        in_specs=[
            _row_spec(TM, D_MODEL), _mod_spec(0, TM), _full_spec((1, D_MODEL)),
            _full_spec(w_main.shape), _full_spec(b_main.shape), _full_spec(w_gate.shape), _full_spec(b_gate.shape),
        ],
        out_specs=[_row_spec(TM, cols), _row_spec(TM, A_GATE_COLS)],
        out_shape=[jax.ShapeDtypeStruct((rows, cols), F32), jax.ShapeDtypeStruct((rows, A_GATE_COLS), F32)],
        compiler_params=_cparams("parallel"),
        name="mlstm_in_proj",
    )(x, mod, g1, w_main, b_main, w_gate, b_gate)


def _mlstm_out_ff_kernel(hf_ref, hb_ref, o_ref, x_ref, mod_ref, modn_ref, gh_ref, wo_ref, g2_ref,
                         wgu_ref, wdn_ref, gn_ref, x_out_ref, hn_out_ref):
    hs = hf_ref[...] + hb_ref[...]
    parts = []
    for h in range(A_HEADS):
        seg = hs[:, h * A_DV:(h + 1) * A_DV]
        parts.append(seg * lax.rsqrt(jnp.mean(seg * seg, axis=-1, keepdims=True) + RMS_EPS))
    y = (jnp.concatenate(parts, axis=1) * gh_ref[...]) * jax.nn.sigmoid(o_ref[...])
    x1 = x_ref[...] + mod_ref[2:3, :] * _dot(y.astype(BF16), wo_ref[...])
    hn2 = _norm_mod(x1, g2_ref[...], mod_ref[4:5, :], mod_ref[3:4, :])
    gu = _dot(hn2.astype(BF16), wgu_ref[...])
    act = jax.nn.silu(gu[:, :D_FF]) * gu[:, D_FF:]
    x2 = x1 + mod_ref[5:6, :] * _dot(act.astype(BF16), wdn_ref[...])
    x_out_ref[...] = x2
    hn_out_ref[...] = _norm_mod(x2, gn_ref[...], modn_ref[1:2, :], modn_ref[0:1, :])


def _mlstm_out_ff(hf, hb, z, x, mod, g_head, w_out, g2, w_gu, w_down, g1_next):
    rows = x.shape[0]
    return pl.pallas_call(
        _mlstm_out_ff_kernel,
        grid=(rows // TM,),
        in_specs=[
            _row_spec(TM, A_V), _row_spec(TM, A_V), _row_spec(TM, A_V, 2), _row_spec(TM, D_MODEL),
            _mod_spec(0, TM), _mod_spec(1, TM), _full_spec((1, A_V)), _full_spec(w_out.shape),
            _full_spec((1, D_MODEL)), _full_spec(w_gu.shape), _full_spec(w_down.shape), _full_spec((1, D_MODEL)),
        ],
        out_specs=[_row_spec(TM, D_MODEL), _row_spec(TM, D_MODEL)],
        out_shape=[jax.ShapeDtypeStruct((rows, D_MODEL), F32)] * 2,
        compiler_params=_cparams("parallel"),
        name="mlstm_out_ff",
    )(hf, hb, z, x, mod, mod, g_head, w_out, g2, w_gu, w_down, g1_next)


def _rwkv_in_kernel(hn_ref, hs_ref, mu_ref, wrkv_ref, w1_ref, w2_ref, w0_ref, a1_ref, a2_ref, a0_ref,
                    g1_ref, g2_ref, r_ref, k_ref, v_ref, lw0_ref, lw1_ref, ia0_ref, ia1_ref, g_ref):
    h = hn_ref[...]
    xx = hs_ref[...] - h

    def mix(c):
        return (h + xx * mu_ref[c:c + 1, :]).astype(BF16)

    r_ref[...] = _dot(mix(0), wrkv_ref[0])
    k_ref[...] = _dot(mix(1), wrkv_ref[1])
    v_ref[...] = _dot(mix(2), wrkv_ref[2])
    lane = lax.broadcasted_iota(jnp.int32, (h.shape[0], 2 * 64), 1)
    dir0 = lane < 64
    t = jnp.tanh(_dot(mix(3), w1_ref[...]))
    lw0_ref[...] = w0_ref[0:1, :] + _dot(jnp.where(dir0, t, 0.0).astype(BF16), w2_ref[...])
    lw1_ref[...] = w0_ref[1:2, :] + _dot(jnp.where(dir0, 0.0, t).astype(BF16), w2_ref[...])
    u = _dot(mix(4), a1_ref[...])
    ia0_ref[...] = a0_ref[0:1, :] + _dot(jnp.where(dir0, u, 0.0).astype(BF16), a2_ref[...])
    ia1_ref[...] = a0_ref[1:2, :] + _dot(jnp.where(dir0, 0.0, u).astype(BF16), a2_ref[...])
    g_ref[...] = _dot(jax.nn.sigmoid(_dot(mix(5), g1_ref[...])).astype(BF16), g2_ref[...])


def _rwkv_in(hn, hshift, mu, w_rkv, w1, w2, w0, a1, a2, a0, g1, g2):
    rows = hn.shape[0]
    ins = [mu, w_rkv, w1, w2, w0, a1, a2, a0, g1, g2]
    return pl.pallas_call(
        _rwkv_in_kernel,
        grid=(rows // TM,),
        in_specs=[_row_spec(TM, D_MODEL), _row_spec(TM, D_MODEL)] + [_full_spec(a.shape) for a in ins],
        out_specs=[_row_spec(TM, D_MODEL)] * 8,
        out_shape=[jax.ShapeDtypeStruct((rows, D_MODEL), F32)] * 8,
        compiler_params=_cparams("parallel"),
        name="rwkv_in_proj",
    )(hn, hshift, *ins)


def _rwkv_out_kernel(yf_ref, yb_ref, g_ref, x_ref, mod_ref, lng_ref, lnb_ref, wo_ref, x_out_ref):
    y = yf_ref[...] + yb_ref[...]
    wi = lax.broadcasted_iota(jnp.int32, (B_PAIR_W, B_PAIR_W), 0)
    wj = lax.broadcasted_iota(jnp.int32, (B_PAIR_W, B_PAIR_W), 1)
    bd_mean = jnp.where((wi >= B_HEAD_DIM) == (wj >= B_HEAD_DIM), 1.0 / B_HEAD_DIM, 0.0)
    parts = []
    for p in range(B_PAIRS):
        seg = y[:, p * B_PAIR_W:(p + 1) * B_PAIR_W]
        cen = seg - _dot(seg, bd_mean, HIGHEST)
        var = _dot(cen * cen, bd_mean, HIGHEST)
        parts.append(cen * lax.rsqrt(var + B_LN_EPS))
    yn = jnp.concatenate(parts, axis=1) * lng_ref[...] + lnb_ref[...]
    out = _dot((yn * g_ref[...]).astype(BF16), wo_ref[...])
    x_out_ref[...] = x_ref[...] + mod_ref[2:3, :] * out


def _rwkv_out(yf, yb, g, x, mod, ln_g, ln_b, w_out):
    rows = x.shape[0]
    return pl.pallas_call(
        _rwkv_out_kernel,
        grid=(rows // TM,),
        in_specs=[_row_spec(TM, D_MODEL)] * 4 + [
            _mod_spec(1, TM), _full_spec((1, D_MODEL)), _full_spec((1, D_MODEL)), _full_spec(w_out.shape)],
        out_specs=_row_spec(TM, D_MODEL),
        out_shape=jax.ShapeDtypeStruct((rows, D_MODEL), F32),
        compiler_params=_cparams("parallel"),
        name="rwkv_out_proj",
    )(yf, yb, g, x, mod, ln_g, ln_b, w_out)


def _moe_kernel(x_ref, mod_ref, g2_ref, wr_ref, br_ref, wgu_ref, wdn_ref, gf_ref, out_ref,
                hn_sc, comb_sc, acc_sc):
    e = pl.program_id(1)
    lane = lax.broadcasted_iota(jnp.int32, comb_sc.shape, 1)

    @pl.when(e == 0)
    def _():
        hn = _norm_mod(x_ref[...], g2_ref[...], mod_ref[4:5, :], mod_ref[3:4, :])
        hn_sc[...] = hn.astype(BF16)
        logits = _dot(hn, wr_ref[...], HIGHEST) + br_ref[...]
        m1 = jnp.max(logits, axis=1, keepdims=True)
        i1 = jnp.min(jnp.where(logits == m1, lane, ROUTER_COLS), axis=1, keepdims=True)
        rest = jnp.where(lane == i1, -jnp.inf, logits)
        m2 = jnp.max(rest, axis=1, keepdims=True)
        i2 = jnp.min(jnp.where(rest == m2, lane, ROUTER_COLS), axis=1, keepdims=True)
        e2 = jnp.exp(m2 - m1)
        den = 1.0 + e2
        comb_sc[...] = jnp.where(lane == i1, 1.0 / den, 0.0) + jnp.where(lane == i2, e2 / den, 0.0)
        acc_sc[...] = jnp.zeros_like(acc_sc)

    gu = _dot(hn_sc[...], wgu_ref[...])
    act = jax.nn.silu(gu[:, :D_FF_EXPERT]) * gu[:, D_FF_EXPERT:]
    ye = _dot(act.astype(BF16), wdn_ref[...])
    ce = jnp.sum(jnp.where(lane == e, comb_sc[...], 0.0), axis=1, keepdims=True)
    acc_sc[...] += ce * ye

    @pl.when(e == N_EXPERTS - 1)
    def _():
        x4 = x_ref[...] + mod_ref[5:6, :] * acc_sc[...]
        out_ref[...] = (x4 * lax.rsqrt(jnp.mean(x4 * x4, axis=-1, keepdims=True) + RMS_EPS)) * gf_ref[...]


def _moe(x, mod, g2, w_router, b_router, w_gu, w_down, g_final):
    rows = x.shape[0]
    tm = TM_MOE
    return pl.pallas_call(
        _moe_kernel,
        grid=(rows // tm, N_EXPERTS),
        in_specs=[
            _row_spec(tm, D_MODEL), _mod_spec(1, tm), _full_spec((1, D_MODEL)),
            _full_spec(w_router.shape), _full_spec(b_router.shape),
            pl.BlockSpec((None, D_MODEL, 2 * D_FF_EXPERT), lambda i, e: (e, 0, 0)),
            pl.BlockSpec((None, D_FF_EXPERT, D_MODEL), lambda i, e: (e, 0, 0)),
            _full_spec((1, D_MODEL)),
        ],
        out_specs=_row_spec(tm, D_MODEL),
        out_shape=jax.ShapeDtypeStruct((rows, D_MODEL), F32),
        scratch_shapes=[
            pltpu.VMEM((tm, D_MODEL), BF16),
            pltpu.VMEM((tm, ROUTER_COLS), F32),
            pltpu.VMEM((tm, D_MODEL), F32),
        ],
        compiler_params=_cparams("parallel", "arbitrary"),
        name="moe_final",
    )(x, mod, g2, w_router, b_router, w_gu, w_down, g_final)


def _shift_1d(x):
    half = x.shape[-1] // 2
    prev = jnp.pad(x[:, :-1, :half], ((0, 0), (1, 0), (0, 0)))
    nxt = jnp.pad(x[:, 1:, half:], ((0, 0), (0, 1), (0, 0)))
    return jnp.concatenate([prev, nxt], axis=-1)


def _shift_grid(x):
    b, t, d = x.shape
    g = x.reshape(b, t // GRID_W, GRID_W, d)
    q = d // 4
    left = jnp.pad(g[:, :, :-1, :q], ((0, 0), (0, 0), (1, 0), (0, 0)))
    right = jnp.pad(g[:, :, 1:, q:2 * q], ((0, 0), (0, 0), (0, 1), (0, 0)))
    up = jnp.pad(g[:, :-1, :, 2 * q:3 * q], ((0, 0), (1, 0), (0, 0), (0, 0)))
    down = jnp.pad(g[:, 1:, :, 3 * q:], ((0, 0), (0, 1), (0, 0), (0, 0)))
    return jnp.concatenate([left, right, up, down], axis=-1).reshape(b, t, d)


def _pad_cols(a, cols, value=0.0):
    return jnp.pad(a, ((0, 0), (0, cols - a.shape[1])), constant_values=value)


def kernel(x_prompt, x_sample, state_mlstm_C, state_mlstm_n, state_mlstm_m, state_rwkv_S, c, c_ctx, w_ada, b_ada, g_norm1, g_norm2, w_m_in, b_m_in, g_m_head, w_m_out, mu_rwkv, w_rkv, w_decay0, w_decay1, w_decay2, w_iclr0, w_iclr1, w_iclr2, w_gate1, w_gate2, k_k, k_a, r_k, g_ln_x, b_ln_x, w_rwkv_out, w_ff_gu, w_ff_down, w_router, b_router, w_exp_gu, w_exp_down, g_final):
    nb, seq, d = x_prompt.shape
    db, dseq, _ = x_sample.shape
    n_ctx = nb * seq
    n_lat = db * dseq
    assert d == D_MODEL and n_ctx == GROUP_ROWS and dseq == GROUP_ROWS

    x = jnp.concatenate([x_prompt.reshape(n_ctx, d), x_sample.reshape(n_lat, d)], axis=0)
    cond = jnp.concatenate([c_ctx[None, :], c], axis=0)
    n_groups = cond.shape[0]
    cond = jnp.pad(cond, ((0, (-n_groups) % 8), (0, 0)))
    mod = _ada(cond, w_ada, b_ada).reshape(w_ada.shape[0], cond.shape[0], N_ADA, d)
    row = lambda a: a.reshape(1, -1)

    n_main = 2 * A_QK + 2 * A_V
    w_in = w_m_in[0]
    z, gates = _in_proj(x, mod, row(g_norm1[0]), w_in[:, :n_main].astype(BF16), row(b_m_in[0][:n_main]),
                        _pad_cols(w_in[:, n_main:], A_GATE_COLS), _pad_cols(row(b_m_in[0][n_main:]), A_GATE_COLS))
    gates_t = gates[:, :4 * A_HEADS].T
    zc = jnp.zeros((nb, A_HEADS, A_DK, A_DV), F32)
    zn = jnp.zeros((nb, A_HEADS, A_DK), F32)
    zm = jnp.zeros((nb, A_HEADS, 1), F32)
    h_dirs, ctx_state = [], []
    for dirn, rev in ((0, False), (1, True)):
        h_c, cf, nf, mf = _mlstm_scan(z, gates, gates_t, zc, zn, zm, row0=0, nseq=nb, seqlen=seq, reverse=rev)
        h_l, _, _, _ = _mlstm_scan(z, gates, gates_t, state_mlstm_C[:, 0, dirn], state_mlstm_n[:, 0, dirn],
                                   state_mlstm_m[:, 0, dirn][..., None], row0=n_ctx, nseq=db, seqlen=dseq, reverse=rev)
        h_dirs.append(jnp.concatenate([h_c, h_l], axis=0))
        ctx_state.append((cf, nf, mf[..., 0]))
    new_c = jnp.stack([s[0] for s in ctx_state], axis=1)[:, None]
    new_n = jnp.stack([s[1] for s in ctx_state], axis=1)[:, None]
    new_m = jnp.stack([s[2] for s in ctx_state], axis=1)[:, None]
    x, hn = _mlstm_out_ff(h_dirs[0], h_dirs[1], z, x, mod, row(g_m_head[0]), w_m_out[0].astype(BF16),
                          row(g_norm2[0]), w_ff_gu[0].astype(BF16), w_ff_down[0].astype(BF16), row(g_norm1[1]))

    hshift = jnp.concatenate([
        _shift_1d(hn[:n_ctx].reshape(nb, seq, d)).reshape(n_ctx, d),
        _shift_grid(hn[n_ctx:].reshape(db, dseq, d)).reshape(n_lat, d)], axis=0)
    cat_cols = lambda w: jnp.concatenate([w[0], w[1]], axis=1).astype(BF16)
    cat_rows = lambda w: jnp.concatenate([w[0], w[1]], axis=0).astype(BF16)
    r, k, v, lw0, lw1, ia0, ia1, g = _rwkv_in(
        hn, hshift, mu_rwkv[0], w_rkv[0].astype(BF16), cat_cols(w_decay1[0]), cat_rows(w_decay2[0]), w_decay0[0],
        cat_cols(w_iclr1[0]), cat_rows(w_iclr2[0]), w_iclr0[0], w_gate1[0].astype(BF16), w_gate2[0].astype(BF16))
    zs = jnp.zeros((nb, B_PAIRS, B_PAIR_W, B_PAIR_W), F32)
    y_dirs, ctx_s = [], []
    for dirn, rev, lw, ia in ((0, False, lw0, ia0), (1, True, lw1, ia1)):
        args = (r, k, v, lw, ia, row(k_k[0]), row(k_a[0]), row(r_k[0]))
        y_c, sf = _rwkv_scan(*args, zs, row0=0, nseq=nb, seqlen=seq, reverse=rev)
        y_l, _ = _rwkv_scan(*args, _pair_blockdiag(state_rwkv_S[:, 0, dirn]), row0=n_ctx, nseq=db, seqlen=dseq,
                            reverse=rev)
        y_dirs.append(jnp.concatenate([y_c, y_l], axis=0))
        ctx_s.append(_pair_unblock(sf))
    new_s = jnp.stack(ctx_s, axis=1)[:, None]
    x = _rwkv_out(y_dirs[0], y_dirs[1], g, x, mod, row(g_ln_x[0]), row(b_ln_x[0]), w_rwkv_out[0].astype(BF16))
    y = _moe(x, mod, row(g_norm2[1]), _pad_cols(w_router[0], ROUTER_COLS),
             _pad_cols(row(b_router[0]), ROUTER_COLS, NEG_BIG), w_exp_gu[0].astype(BF16), w_exp_down[0].astype(BF16),
             row(g_final))
    return (y[:n_ctx].reshape(nb, seq, d), y[n_ctx:].reshape(db, dseq, d), new_c, new_n, new_m, new_s)
```

```python
import functools

import jax
import jax.numpy as jnp
from jax import lax
from jax.experimental import pallas as pl
from jax.experimental.pallas import tpu as pltpu

F32 = jnp.float32
BF16 = jnp.bfloat16
HIGHEST = lax.Precision.HIGHEST

D_MODEL = 1024
GROUP_ROWS = 4096
N_ADA = 6
RMS_EPS = 1e-6
GRID_W = 64
A_HEADS = 8
A_DK = 64
A_DV = 128
A_QK = A_HEADS * A_DK
A_V = A_HEADS * A_DV
A_CHUNK = 128
A_GATE_COLS = 128
A_SCAN_ROWS = 256
B_HEADS = 16
B_HEAD_DIM = 64
B_PAIRS = B_HEADS // 2
B_PAIR_W = 2 * B_HEAD_DIM
B_CHUNK = 64
B_DECAY_SCALE = 0.606531
B_LN_EPS = 64e-5
B_SCAN_ROWS = 128
D_FF = 2816
N_EXPERTS = 8
D_FF_EXPERT = D_FF // 2
ROUTER_COLS = 128
NEG_BIG = -1e30

VMEM_LIMIT = 56 * 1024 * 1024


def _cparams(*sem):
    return pltpu.CompilerParams(dimension_semantics=sem, vmem_limit_bytes=VMEM_LIMIT)


def _dot(a, b, precision=None):
    return jnp.dot(a, b, precision=precision, preferred_element_type=F32)


def _dot_nt(a, b, precision=None):
    return lax.dot_general(a, b, (((1,), (1,)), ((), ())), precision=precision, preferred_element_type=F32)


def _dot_tn(a, b, precision=None):
    return lax.dot_general(a, b, (((0,), (0,)), ((), ())), precision=precision, preferred_element_type=F32)


_NN = (((1,), (0,)), ((), ()))
_NT = (((1,), (1,)), ((), ()))
_TN = (((0,), (0,)), ((), ()))


def _bf16_terms(a, n):
    terms = []
    for i in range(n):
        t = a.astype(BF16)
        terms.append(t)
        if i + 1 < n:
            a = a - t.astype(F32)
    return terms


def _mxu(a, b, dims=_NN, na=1, nb=1):
    at = _bf16_terms(a, na)
    bt = _bf16_terms(b, nb)
    out = None
    for i, x in enumerate(at):
        for j, y in enumerate(bt):
            if i + j < max(na, nb):
                t = lax.dot_general(x, y, dims, preferred_element_type=F32)
                out = t if out is None else out + t
    return out


def _norm_mod(x, g, sc, sh):
    y = x * lax.rsqrt(jnp.mean(x * x, axis=-1, keepdims=True) + RMS_EPS)
    return (y * g) * (1.0 + sc) + sh


def _log_sigmoid(x):
    return jnp.minimum(x, 0.0) - jnp.log1p(jnp.exp(-jnp.abs(x)))


def _ada_kernel(cond_ref, w_ref, b_ref, out_ref):
    c = cond_ref[...]
    out_ref[...] = _dot(c * jax.nn.sigmoid(c), w_ref[...], HIGHEST) + b_ref[...]


def _ada(cond_pad, w_ada, b_ada):
    depth, d, cols = w_ada.shape
    rows = cond_pad.shape[0]
    tn = 1536
    return pl.pallas_call(
        _ada_kernel,
        grid=(depth, cols // tn),
        in_specs=[
            pl.BlockSpec((rows, d), lambda l, j: (0, 0)),
            pl.BlockSpec((None, d, tn), lambda l, j: (l, 0, j)),
            pl.BlockSpec((None, 1, tn), lambda l, j: (l, 0, j)),
        ],
        out_specs=pl.BlockSpec((None, rows, tn), lambda l, j: (l, 0, j)),
        out_shape=jax.ShapeDtypeStruct((depth, rows, cols), F32),
        compiler_params=_cparams("parallel", "parallel"),
        name="ada",
    )(cond_pad, w_ada, b_ada.reshape(depth, 1, cols))


def _mlstm_scan_kernel(blk_ref, seq_ref, first_ref, q_ref, k_ref, v_ref, gc_ref, gr_ref, c0_ref, n0_ref, m0_ref,
                       h_ref, cf_ref, nf_ref, mf_ref, c_sc, n_sc, m_sc, *, reverse, n_sub):
    L = A_CHUNK

    @pl.when(first_ref[pl.program_id(0)] == 1)
    def _():
        c_sc[...] = c0_ref[...]
        n_sc[...] = n0_ref[...]
        m_sc[...] = m0_ref[...]

    row = lax.broadcasted_iota(jnp.int32, (L, L), 0)
    col = lax.broadcasted_iota(jnp.int32, (L, L), 1)
    in_win = (col >= row) if reverse else (col <= row)
    in_win_t = (row >= col) if reverse else (row <= col)
    d0 = A_HEADS if reverse else 0

    subs = list(range(n_sub - 1, -1, -1) if reverse else range(n_sub))
    units = [(sub * L, h) for sub in subs for h in range(A_HEADS)]
    fcol = {sub: _log_sigmoid(gc_ref[sub * L:(sub + 1) * L, :]) for sub in subs}
    frow = {sub: _log_sigmoid(gr_ref[:, sub * L:(sub + 1) * L]) for sub in subs}

    ci_l = [d0 + h for _, h in units]
    cf_l = [2 * A_HEADS + d0 + h for _, h in units]
    i_col_l = [gc_ref[r0:r0 + L, ci:ci + 1] for (r0, _), ci in zip(units, ci_l)]
    i_row_l = [gr_ref[ci:ci + 1, r0:r0 + L] for (r0, _), ci in zip(units, ci_l)]
    f_col_l = [fcol[r0 // L][:, cf:cf + 1] for (r0, _), cf in zip(units, cf_l)]
    f_row_l = [frow[r0 // L][cf:cf + 1, :] for (r0, _), cf in zip(units, cf_l)]
    b_col_l = [jnp.sum(jnp.where(in_win, f, 0.0), axis=1, keepdims=True) for f in f_row_l]
    b_row_l = [jnp.sum(jnp.where(in_win_t, f, 0.0), axis=0, keepdims=True) for f in f_col_l]
    tot_l = [jnp.sum(f, axis=1, keepdims=True) for f in f_row_l]
    logw_l = [jnp.where(in_win, bc - br + ir, -jnp.inf) for bc, br, ir in zip(b_col_l, b_row_l, i_row_l)]
    mx_l = [jnp.max(lw, axis=1, keepdims=True) for lw in logw_l]
    p_l = [jnp.exp(lw - mx) for lw, mx in zip(logw_l, mx_l)]
    rep = lambda x: jnp.broadcast_to(x, (x.shape[0], A_DV))
    b_col_l = [rep(b) for b in b_col_l]
    mx_l = [rep(m) for m in mx_l]
    tot_l = [rep(t) for t in tot_l]
    log_g_l = [t - bc + rep(ic) for t, bc, ic in zip(tot_l, b_col_l, i_col_l)]
    mg_l = [jnp.max(lg, axis=0, keepdims=True) for lg in log_g_l]
    wg_l = [jnp.exp(lg - mg)[:, :A_DK] for lg, mg in zip(log_g_l, mg_l)]
    q_l = [q_ref[r0:r0 + L, h * A_DK:(h + 1) * A_DK] for r0, h in units]
    k_l = [k_ref[r0:r0 + L, h * A_DK:(h + 1) * A_DK] * (A_DK ** -0.5) for r0, h in units]
    v_l = [v_ref[r0:r0 + L, h * A_DV:(h + 1) * A_DV].astype(BF16) for r0, h in units]
    qk_l = [_mxu(q, k, _NT) for q, k in zip(q_l, k_l)]
    sp_l = [qk * p for qk, p in zip(qk_l, p_l)]
    ssum_l = [rep(jnp.sum(sp, axis=1, keepdims=True)) for sp in sp_l]
    sv_l = [_mxu(sp, v) for sp, v in zip(sp_l, v_l)]
    kw_l = [k * wg for k, wg in zip(k_l, wg_l)]
    ksum_l = [jnp.sum(kw, axis=0, keepdims=True) for kw in kw_l]
    kv_l = [_mxu(kw, v, _TN) for kw, v in zip(kw_l, v_l)]

    for i0 in range(0, len(units), A_HEADS):
        idx = range(i0, i0 + A_HEADS)
        heads = range(A_HEADS)
        m_prev = [m_sc[h:h + 1, :] for h in heads]
        c_prev = [c_sc[h] for h in heads]
        n_prev = [n_sc[h:h + 1, :] for h in heads]
        qc = [_mxu(q_l[i], c) for i, c in zip(idx, c_prev)]
        qn = [rep(jnp.sum(q_l[i] * n, axis=1, keepdims=True)) for i, n in zip(idx, n_prev)]
        inter = [b_col_l[i] + m for i, m in zip(idx, m_prev)]
        m_row = [jnp.maximum(it, mx_l[i]) for i, it in zip(idx, inter)]
        corr = [jnp.exp(mx_l[i] - mr) for i, mr in zip(idx, m_row)]
        w_inter = [jnp.exp(it - mr) for it, mr in zip(inter, m_row)]
        floor = [jnp.exp(-mr) for mr in m_row]
        num = [cr * sv_l[i] + wi * x for i, cr, wi, x in zip(idx, corr, w_inter, qc)]
        den = [cr * ssum_l[i] + wi * x for i, cr, wi, x in zip(idx, corr, w_inter, qn)]
        m_new = [jnp.maximum(tot_l[i] + m, mg_l[i]) for i, m in zip(idx, m_prev)]
        decay = [jnp.exp(tot_l[i] + m - mn) for i, m, mn in zip(idx, m_prev, m_new)]
        gain = [jnp.exp(mg_l[i] - mn) for i, mn in zip(idx, m_new)]
        for i, h in zip(idx, heads):
            r0 = units[i][0]
            h_ref[r0:r0 + L, h * A_DV:(h + 1) * A_DV] = num[h] / jnp.maximum(jnp.abs(den[h]), floor[h])
            c_sc[h] = decay[h] * c_prev[h] + gain[h] * kv_l[i]
            n_sc[h:h + 1, :] = decay[h][:, :A_DK] * n_prev[h] + gain[h][:, :A_DK] * ksum_l[i]
            m_sc[h:h + 1, :] = m_new[h]

    cf_ref[...] = c_sc[...]
    nf_ref[...] = n_sc[...]
    mf_ref[...] = m_sc[...]


def _scan_schedule(seq_lens, cb, reverse):
    blk, seq, first = [], [], []
    base = 0
    for s, n in enumerate(seq_lens):
        nb = n // cb
        order = range(nb - 1, -1, -1) if reverse else range(nb)
        for j, b in enumerate(order):
            blk.append(base + b)
            seq.append(s)
            first.append(1 if j == 0 else 0)
        base += nb
    as_i32 = lambda a: jnp.asarray(a, dtype=jnp.int32)
    return as_i32(blk), as_i32(seq), as_i32(first)


def _mlstm_scan(z, gates, gates_t, c0, n0, m0, *, seq_lens, reverse):
    cb = A_SCAN_ROWS
    nseq = len(seq_lens)
    rows = sum(seq_lens)
    blk, seq, first = _scan_schedule(seq_lens, cb, reverse)
    tile = lambda cols, col_block: pl.BlockSpec((cb, cols), lambda t, blk, seq, first: (blk[t], col_block))
    state = lambda *shape: pl.BlockSpec((None,) + shape, lambda t, blk, seq, first: (seq[t],) + (0,) * len(shape))
    grid_spec = pltpu.PrefetchScalarGridSpec(
        num_scalar_prefetch=3,
        grid=(blk.shape[0],),
        in_specs=[
            tile(A_QK, 0), tile(A_QK, 1), tile(A_V, 1), tile(A_GATE_COLS, 0),
            pl.BlockSpec((4 * A_HEADS, cb), lambda t, blk, seq, first: (0, blk[t])),
            state(A_HEADS, A_DK, A_DV), state(A_HEADS, A_DK), state(A_HEADS, A_DV),
        ],
        out_specs=[tile(A_V, 0), state(A_HEADS, A_DK, A_DV), state(A_HEADS, A_DK), state(A_HEADS, A_DV)],
        scratch_shapes=[
            pltpu.VMEM((A_HEADS, A_DK, A_DV), F32),
            pltpu.VMEM((A_HEADS, A_DK), F32),
            pltpu.VMEM((A_HEADS, A_DV), F32),
        ],
    )
    kern = functools.partial(_mlstm_scan_kernel, reverse=reverse, n_sub=cb // A_CHUNK)
    return pl.pallas_call(
        kern,
        grid_spec=grid_spec,
        out_shape=[
            jax.ShapeDtypeStruct((rows, A_V), F32),
            jax.ShapeDtypeStruct((nseq, A_HEADS, A_DK, A_DV), F32),
            jax.ShapeDtypeStruct((nseq, A_HEADS, A_DK), F32),
            jax.ShapeDtypeStruct((nseq, A_HEADS, A_DV), F32),
        ],
        compiler_params=_cparams("arbitrary"),
        name="mlstm_scan_bwd" if reverse else "mlstm_scan_fwd",
    )(blk, seq, first, z, z, z, gates, gates_t, c0, n0, m0)


def _rwkv_scan_kernel(blk_ref, seq_ref, first_ref, r_ref, k_ref, v_ref, lw_ref, a_ref, kk_ref, ka_ref, rk_ref,
                      s0_ref, y_ref, sf_ref, s_sc, *, reverse, n_sub):
    L = B_CHUNK
    W = B_PAIR_W
    R = 2 * L

    @pl.when(first_ref[pl.program_id(0)] == 1)
    def _():
        s_sc[...] = s0_ref[...]

    lane = lax.broadcasted_iota(jnp.int32, (L, W), 1)
    head0 = lane < B_HEAD_DIM
    ri = lax.broadcasted_iota(jnp.int32, (R, R), 0)
    ci = lax.broadcasted_iota(jnp.int32, (R, R), 1)
    same_head = (ri >= L) == (ci >= L)
    strict = same_head & ((ri < ci) if reverse else (ri > ci))
    incl = same_head & ((ri <= ci) if reverse else (ri >= ci))
    eye = (ri == ci).astype(F32)
    wi = lax.broadcasted_iota(jnp.int32, (W, W), 0)
    wj = lax.broadcasted_iota(jnp.int32, (W, W), 1)
    bd = (wi >= B_HEAD_DIM) == (wj >= B_HEAD_DIM)
    bd_ones = jnp.where(bd, 1.0, 0.0).astype(BF16)
    ti = lax.broadcasted_iota(jnp.int32, (L, L), 0)
    tj = lax.broadcasted_iota(jnp.int32, (L, L), 1)
    tri = jnp.where((tj >= ti) if reverse else (tj <= ti), 1.0, 0.0).astype(BF16)

    def stack_masked(x):
        return jnp.concatenate([jnp.where(head0, x, 0.0), jnp.where(head0, 0.0, x)], axis=0)

    def stack_dup(x):
        return jnp.concatenate([x, x], axis=0)

    subs = list(range(n_sub - 1, -1, -1) if reverse else range(n_sub))
    units = [(sub * L, slice(p * W, (p + 1) * W)) for sub in subs for p in range(B_PAIRS)]
    nu = len(units)

    def each(fn, *lists):
        return [fn(*args) for args in zip(*lists)]

    r_l = [r_ref[r0:r0 + L, sl] for r0, sl in units]
    k_l = [k_ref[r0:r0 + L, sl] for r0, sl in units]
    v_l = [v_ref[r0:r0 + L, sl] for r0, sl in units]
    lw_l = [-B_DECAY_SCALE * jax.nn.sigmoid(lw_ref[r0:r0 + L, sl]) for r0, sl in units]
    a_l = [jax.nn.sigmoid(a_ref[r0:r0 + L, sl]) for r0, sl in units]
    kkr_l = [k * kk_ref[:, sl] for k, (_, sl) in zip(k_l, units)]
    kd_l = [k * (1.0 + (a - 1.0) * ka_ref[:, sl]) for k, a, (_, sl) in zip(k_l, a_l, units)]
    ss_l = [_mxu(kkr * kkr, bd_ones, na=3) for kkr in kkr_l]
    bsum_l = [_mxu(r * kd * rk_ref[:, sl], bd_ones, na=3) for r, kd, (_, sl) in zip(r_l, kd_l, units)]
    c_inc_l = [_mxu(tri, lw, nb=3) for lw in lw_l]
    kk_l = each(lambda kkr, ss: kkr / jnp.maximum(jnp.sqrt(ss), 1e-12), kkr_l, ss_l)
    bv_l = each(lambda kk, a: kk * a, kk_l, a_l)
    bonus_l = each(lambda bs, v: bs * v, bsum_l, v_l)
    c_mid_l = [c[L // 2:L // 2 + 1, :] for c in c_inc_l]
    tot_l = [(c[0:1, :] if reverse else c[L - 1:L, :]) for c in c_inc_l]
    x_all_l = each(lambda kk, r, c, lw: jnp.concatenate(
        [stack_masked(-kk * jnp.exp(c - lw)), stack_masked(r * jnp.exp(c))], axis=0), kk_l, r_l, c_inc_l, lw_l)
    xa_l = each(lambda x, cm: (x * jnp.exp(-cm)).astype(BF16), x_all_l, c_mid_l)
    x_all_l = [x.astype(BF16) for x in x_all_l]

    def decayed(bv, kd, e):
        return jnp.concatenate([stack_dup(bv * e), stack_dup(kd * e)], axis=0).astype(BF16)

    y_all_l = each(lambda bv, kd, c, cm: decayed(bv, kd, jnp.exp(cm - c)), bv_l, kd_l, c_inc_l, c_mid_l)
    y_end_l = each(lambda bv, kd, c, t: decayed(bv, kd, jnp.exp(t - c)), bv_l, kd_l, c_inc_l, tot_l)
    v_s_l = [stack_masked(v).astype(BF16) for v in v_l]

    a_all_l = each(lambda x, y: _mxu(x, y, _NT), xa_l, y_all_l)
    a_ab_l = [jnp.where(strict, a[:R, :R], 0.0) for a in a_all_l]
    a_ak_l = [jnp.where(strict, a[:R, R:], 0.0).astype(BF16) for a in a_all_l]
    a_r_l = [jnp.concatenate([jnp.where(incl, a[R:, :R], 0.0), jnp.where(incl, a[R:, R:], 0.0)],
                             axis=1).astype(BF16) for a in a_all_l]
    akv_l = each(lambda a, v: _mxu(a, v, _NN), a_ak_l, v_s_l)

    t_inv_l = [eye + a for a in a_ab_l]
    pw_l = [_mxu(a, a, _NN) for a in a_ab_l]
    for _ in range(4):
        both_l = each(lambda pw, t: _mxu(pw, jnp.concatenate([pw, t], axis=1), _NN), pw_l, t_inv_l)
        pw_l = [b[:, :R] for b in both_l]
        t_inv_l = each(lambda t, b: t + b[:, R:], t_inv_l, both_l)
    t_inv_l = each(lambda t, pw: (t + _mxu(pw, t, _NN)).astype(BF16), t_inv_l, pw_l)

    for i0 in range(0, nu, B_PAIRS):
        idx = list(range(i0, i0 + B_PAIRS))
        s_prev_l = [s_sc[p] for p in range(B_PAIRS)]
        xs_l = [_mxu(x_all_l[i], s, _NT) for i, s in zip(idx, s_prev_l)]
        u_l = [_mxu(t_inv_l[i], xs[:R] + akv_l[i], _NN) for i, xs in zip(idx, xs_l)]
        uv_l = [jnp.concatenate([u.astype(BF16), v_s_l[i]], axis=0) for i, u in zip(idx, u_l)]
        y_s_l = [xs[R:] + _mxu(a_r_l[i], uv, _NN) for i, xs, uv in zip(idx, xs_l, uv_l)]
        for i, y_s in zip(idx, y_s_l):
            r0, sl = units[i]
            y_ref[r0:r0 + L, sl] = y_s[:L] + y_s[L:] + bonus_l[i]
        s_new_l = [s * jnp.exp(tot_l[i]) + _mxu(uv, y_end_l[i], _TN)
                   for i, s, uv in zip(idx, s_prev_l, uv_l)]
        for p, s_new in enumerate(s_new_l):
            s_sc[p] = jnp.where(bd, s_new, 0.0)

    sf_ref[...] = s_sc[...]


def _rwkv_scan(r, k, v, lw, a, k_k, k_a, r_k, s0, *, seq_lens, reverse):
    cb = B_SCAN_ROWS
    nseq = len(seq_lens)
    rows = sum(seq_lens)
    blk, seq, first = _scan_schedule(seq_lens, cb, reverse)
    tile = pl.BlockSpec((cb, D_MODEL), lambda t, blk, seq, first: (blk[t], 0))
    vec = pl.BlockSpec((1, D_MODEL), lambda t, blk, seq, first: (0, 0))
    st = pl.BlockSpec((None, B_PAIRS, B_PAIR_W, B_PAIR_W), lambda t, blk, seq, first: (seq[t], 0, 0, 0))
    grid_spec = pltpu.PrefetchScalarGridSpec(
        num_scalar_prefetch=3,
        grid=(blk.shape[0],),
        in_specs=[tile, tile, tile, tile, tile, vec, vec, vec, st],
        out_specs=[tile, st],
        scratch_shapes=[pltpu.VMEM((B_PAIRS, B_PAIR_W, B_PAIR_W), F32)],
    )
    kern = functools.partial(_rwkv_scan_kernel, reverse=reverse, n_sub=cb // B_CHUNK)
    return pl.pallas_call(
        kern,
        grid_spec=grid_spec,
        out_shape=[
            jax.ShapeDtypeStruct((rows, D_MODEL), F32),
            jax.ShapeDtypeStruct((nseq, B_PAIRS, B_PAIR_W, B_PAIR_W), F32),
        ],
        compiler_params=_cparams("arbitrary"),
        name="rwkv_scan_bwd" if reverse else "rwkv_scan_fwd",
    )(blk, seq, first, r, k, v, lw, a, k_k, k_a, r_k, s0)


def _pair_blockdiag(s):
    n = s.shape[0]
    s = s.reshape(n, B_PAIRS, 2, B_HEAD_DIM, B_HEAD_DIM)
    z = jnp.zeros_like(s[:, :, 0])
    top = jnp.concatenate([s[:, :, 0], z], axis=-1)
    bot = jnp.concatenate([z, s[:, :, 1]], axis=-1)
    return jnp.concatenate([top, bot], axis=-2)


def _pair_unblock(sp):
    n = sp.shape[0]
    h0 = sp[:, :, :B_HEAD_DIM, :B_HEAD_DIM]
    h1 = sp[:, :, B_HEAD_DIM:, B_HEAD_DIM:]
    return jnp.stack([h0, h1], axis=2).reshape(n, B_HEADS, B_HEAD_DIM, B_HEAD_DIM)


TM = 256
TM_MOE = 512


def _mod_spec(layer, tm):
    per_group = GROUP_ROWS // tm
    return pl.BlockSpec((None, None, N_ADA, D_MODEL), lambda i, *_: (layer, i // per_group, 0, 0))


def _row_spec(tm, cols, col_block=0):
    return pl.BlockSpec((tm, cols), lambda i, *_: (i, col_block))


def _full_spec(shape):
    nd = len(shape)
    return pl.BlockSpec(shape, lambda i, *_: (0,) * nd)


def _split_specs(tm, cols, n_first):
    t_first = n_first // tm
    return (pl.BlockSpec((tm, cols), lambda i, *_: (jnp.minimum(i, t_first - 1), 0)),
            pl.BlockSpec((tm, cols), lambda i, *_: (jnp.maximum(i - t_first, 0), 0)))


def _in_proj_kernel(xp_ref, xs_ref, mod_ref, g_ref, w_ref, b_ref, wg_ref, bg_ref, z_ref, gate_ref, *, ctx_tiles):
    x = jnp.where(pl.program_id(0) < ctx_tiles, xp_ref[...], xs_ref[...])
    hn = _norm_mod(x, g_ref[...], mod_ref[1:2, :], mod_ref[0:1, :])
    z_ref[...] = _dot(hn.astype(BF16), w_ref[...]) + b_ref[...]
    gate_ref[...] = _dot(hn, wg_ref[...], HIGHEST) + bg_ref[...]


def _in_proj(xp, xs, mod, g1, w_main, b_main, w_gate, b_gate):
    rows = xp.shape[0] + xs.shape[0]
    cols = w_main.shape[1]
    return pl.pallas_call(
        functools.partial(_in_proj_kernel, ctx_tiles=xp.shape[0] // TM),
        grid=(rows // TM,),
        in_specs=[
            *_split_specs(TM, D_MODEL, xp.shape[0]), _mod_spec(0, TM), _full_spec((1, D_MODEL)),
            _full_spec(w_main.shape), _full_spec(b_main.shape), _full_spec(w_gate.shape), _full_spec(b_gate.shape),
        ],
        out_specs=[_row_spec(TM, cols), _row_spec(TM, A_GATE_COLS)],
        out_shape=[jax.ShapeDtypeStruct((rows, cols), F32), jax.ShapeDtypeStruct((rows, A_GATE_COLS), F32)],
        compiler_params=_cparams("parallel"),
        name="mlstm_in_proj",
    )(xp, xs, mod, g1, w_main, b_main, w_gate, b_gate)


def _mlstm_out_ff_kernel(hf_ref, hb_ref, o_ref, xp_ref, xs_ref, mod_ref, modn_ref, gh_ref, wo_ref, g2_ref,
                         wgu_ref, wdn_ref, gn_ref, x_out_ref, hn_out_ref, *, ctx_tiles):
    x = jnp.where(pl.program_id(0) < ctx_tiles, xp_ref[...], xs_ref[...])
    hs = hf_ref[...] + hb_ref[...]
    parts = []
    for h in range(A_HEADS):
        seg = hs[:, h * A_DV:(h + 1) * A_DV]
        parts.append(seg * lax.rsqrt(jnp.mean(seg * seg, axis=-1, keepdims=True) + RMS_EPS))
    y = (jnp.concatenate(parts, axis=1) * gh_ref[...]) * jax.nn.sigmoid(o_ref[...])
    x1 = x + mod_ref[2:3, :] * _dot(y.astype(BF16), wo_ref[...])
    hn2 = _norm_mod(x1, g2_ref[...], mod_ref[4:5, :], mod_ref[3:4, :])
    gu = _dot(hn2.astype(BF16), wgu_ref[...])
    act = jax.nn.silu(gu[:, :D_FF]) * gu[:, D_FF:]
    x2 = x1 + mod_ref[5:6, :] * _dot(act.astype(BF16), wdn_ref[...])
    x_out_ref[...] = x2
    hn_out_ref[...] = _norm_mod(x2, gn_ref[...], modn_ref[1:2, :], modn_ref[0:1, :])


def _mlstm_out_ff(hf, hb, z, xp, xs, mod, g_head, w_out, g2, w_gu, w_down, g1_next):
    rows = xp.shape[0] + xs.shape[0]
    return pl.pallas_call(
        functools.partial(_mlstm_out_ff_kernel, ctx_tiles=xp.shape[0] // TM),
        grid=(rows // TM,),
        in_specs=[
            _row_spec(TM, A_V), _row_spec(TM, A_V), _row_spec(TM, A_V, 2), *_split_specs(TM, D_MODEL, xp.shape[0]),
            _mod_spec(0, TM), _mod_spec(1, TM), _full_spec((1, A_V)), _full_spec(w_out.shape),
            _full_spec((1, D_MODEL)), _full_spec(w_gu.shape), _full_spec(w_down.shape), _full_spec((1, D_MODEL)),
        ],
        out_specs=[_row_spec(TM, D_MODEL), _row_spec(TM, D_MODEL)],
        out_shape=[jax.ShapeDtypeStruct((rows, D_MODEL), F32)] * 2,
        compiler_params=_cparams("parallel"),
        name="mlstm_out_ff",
    )(hf, hb, z, xp, xs, mod, mod, g_head, w_out, g2, w_gu, w_down, g1_next)


def _rwkv_in_kernel(hn_ref, hs_ref, mu_ref, wrkv_ref, w1_ref, w2_ref, w0_ref, a1_ref, a2_ref, a0_ref,
                    g1_ref, g2_ref, r_ref, k_ref, v_ref, lw0_ref, lw1_ref, ia0_ref, ia1_ref, g_ref):
    h = hn_ref[...]
    xx = hs_ref[...] - h

    def mix(c):
        return (h + xx * mu_ref[c:c + 1, :]).astype(BF16)

    r_ref[...] = _dot(mix(0), wrkv_ref[0])
    k_ref[...] = _dot(mix(1), wrkv_ref[1])
    v_ref[...] = _dot(mix(2), wrkv_ref[2])
    lane = lax.broadcasted_iota(jnp.int32, (h.shape[0], 2 * 64), 1)
    dir0 = lane < 64
    t = jnp.tanh(_dot(mix(3), w1_ref[...]))
    lw0_ref[...] = w0_ref[0:1, :] + _dot(jnp.where(dir0, t, 0.0).astype(BF16), w2_ref[...])
    lw1_ref[...] = w0_ref[1:2, :] + _dot(jnp.where(dir0, 0.0, t).astype(BF16), w2_ref[...])
    u = _dot(mix(4), a1_ref[...])
    ia0_ref[...] = a0_ref[0:1, :] + _dot(jnp.where(dir0, u, 0.0).astype(BF16), a2_ref[...])
    ia1_ref[...] = a0_ref[1:2, :] + _dot(jnp.where(dir0, 0.0, u).astype(BF16), a2_ref[...])
    g_ref[...] = _dot(jax.nn.sigmoid(_dot(mix(5), g1_ref[...])).astype(BF16), g2_ref[...])


def _rwkv_in(hn, hshift, mu, w_rkv, w1, w2, w0, a1, a2, a0, g1, g2):
    rows = hn.shape[0]
    ins = [mu, w_rkv, w1, w2, w0, a1, a2, a0, g1, g2]
    return pl.pallas_call(
        _rwkv_in_kernel,
        grid=(rows // TM,),
        in_specs=[_row_spec(TM, D_MODEL), _row_spec(TM, D_MODEL)] + [_full_spec(a.shape) for a in ins],
        out_specs=[_row_spec(TM, D_MODEL)] * 8,
        out_shape=[jax.ShapeDtypeStruct((rows, D_MODEL), F32)] * 8,
        compiler_params=_cparams("parallel"),
        name="rwkv_in_proj",
    )(hn, hshift, *ins)


def _rwkv_out_kernel(yf_ref, yb_ref, g_ref, x_ref, mod_ref, lng_ref, lnb_ref, wo_ref, x_out_ref):
    y = yf_ref[...] + yb_ref[...]
    wi = lax.broadcasted_iota(jnp.int32, (B_PAIR_W, B_PAIR_W), 0)
    wj = lax.broadcasted_iota(jnp.int32, (B_PAIR_W, B_PAIR_W), 1)
    bd_mean = jnp.where((wi >= B_HEAD_DIM) == (wj >= B_HEAD_DIM), 1.0 / B_HEAD_DIM, 0.0)
    parts = []
    for p in range(B_PAIRS):
        seg = y[:, p * B_PAIR_W:(p + 1) * B_PAIR_W]
        cen = seg - _dot(seg, bd_mean, HIGHEST)
        var = _dot(cen * cen, bd_mean, HIGHEST)
        parts.append(cen * lax.rsqrt(var + B_LN_EPS))
    yn = jnp.concatenate(parts, axis=1) * lng_ref[...] + lnb_ref[...]
    out = _dot((yn * g_ref[...]).astype(BF16), wo_ref[...])
    x_out_ref[...] = x_ref[...] + mod_ref[2:3, :] * out


def _rwkv_out(yf, yb, g, x, mod, ln_g, ln_b, w_out):
    rows = x.shape[0]
    return pl.pallas_call(
        _rwkv_out_kernel,
        grid=(rows // TM,),
        in_specs=[_row_spec(TM, D_MODEL)] * 4 + [
            _mod_spec(1, TM), _full_spec((1, D_MODEL)), _full_spec((1, D_MODEL)), _full_spec(w_out.shape)],
        out_specs=_row_spec(TM, D_MODEL),
        out_shape=jax.ShapeDtypeStruct((rows, D_MODEL), F32),
        compiler_params=_cparams("parallel"),
        name="rwkv_out_proj",
    )(yf, yb, g, x, mod, ln_g, ln_b, w_out)


def _moe_kernel(x_ref, mod_ref, g2_ref, wr_ref, br_ref, wgu_ref, wdn_ref, gf_ref, out_p_ref, out_s_ref,
                hn_sc, comb_sc, acc_sc, *, ctx_tiles):
    e = pl.program_id(1)
    lane = lax.broadcasted_iota(jnp.int32, comb_sc.shape, 1)

    @pl.when(e == 0)
    def _():
        hn = _norm_mod(x_ref[...], g2_ref[...], mod_ref[4:5, :], mod_ref[3:4, :])
        hn_sc[...] = hn.astype(BF16)
        logits = _dot(hn, wr_ref[...], HIGHEST) + br_ref[...]
        m1 = jnp.max(logits, axis=1, keepdims=True)
        i1 = jnp.min(jnp.where(logits == m1, lane, ROUTER_COLS), axis=1, keepdims=True)
        rest = jnp.where(lane == i1, -jnp.inf, logits)
        m2 = jnp.max(rest, axis=1, keepdims=True)
        i2 = jnp.min(jnp.where(rest == m2, lane, ROUTER_COLS), axis=1, keepdims=True)
        e2 = jnp.exp(m2 - m1)
        den = 1.0 + e2
        comb_sc[...] = jnp.where(lane == i1, 1.0 / den, 0.0) + jnp.where(lane == i2, e2 / den, 0.0)
        acc_sc[...] = jnp.zeros_like(acc_sc)

    gu = _dot(hn_sc[...], wgu_ref[...])
    act = jax.nn.silu(gu[:, :D_FF_EXPERT]) * gu[:, D_FF_EXPERT:]
    ye = _dot(act.astype(BF16), wdn_ref[...])
    ce = jnp.sum(jnp.where(lane == e, comb_sc[...], 0.0), axis=1, keepdims=True)
    acc_sc[...] += ce * ye

    def final_norm():
        x4 = x_ref[...] + mod_ref[5:6, :] * acc_sc[...]
        return (x4 * lax.rsqrt(jnp.mean(x4 * x4, axis=-1, keepdims=True) + RMS_EPS)) * gf_ref[...]

    is_ctx = pl.program_id(0) < ctx_tiles

    @pl.when((e == N_EXPERTS - 1) & is_ctx)
    def _():
        out_p_ref[...] = final_norm()

    @pl.when((e == N_EXPERTS - 1) & jnp.logical_not(is_ctx))
    def _():
        out_s_ref[...] = final_norm()


def _moe(x, mod, g2, w_router, b_router, w_gu, w_down, g_final, *, n_ctx):
    rows = x.shape[0]
    tm = TM_MOE
    return pl.pallas_call(
        functools.partial(_moe_kernel, ctx_tiles=n_ctx // tm),
        grid=(rows // tm, N_EXPERTS),
        in_specs=[
            _row_spec(tm, D_MODEL), _mod_spec(1, tm), _full_spec((1, D_MODEL)),
            _full_spec(w_router.shape), _full_spec(b_router.shape),
            pl.BlockSpec((None, D_MODEL, 2 * D_FF_EXPERT), lambda i, e: (e, 0, 0)),
            pl.BlockSpec((None, D_FF_EXPERT, D_MODEL), lambda i, e: (e, 0, 0)),
            _full_spec((1, D_MODEL)),
        ],
        out_specs=list(_split_specs(tm, D_MODEL, n_ctx)),
        out_shape=[jax.ShapeDtypeStruct((n_ctx, D_MODEL), F32), jax.ShapeDtypeStruct((rows - n_ctx, D_MODEL), F32)],
        scratch_shapes=[
            pltpu.VMEM((tm, D_MODEL), BF16),
            pltpu.VMEM((tm, ROUTER_COLS), F32),
            pltpu.VMEM((tm, D_MODEL), F32),
        ],
        compiler_params=_cparams("parallel", "arbitrary"),
        name="moe_final",
    )(x, mod, g2, w_router, b_router, w_gu, w_down, g_final)


def _shift_1d(x):
    half = x.shape[-1] // 2
    prev = jnp.pad(x[:, :-1, :half], ((0, 0), (1, 0), (0, 0)))
    nxt = jnp.pad(x[:, 1:, half:], ((0, 0), (0, 1), (0, 0)))
    return jnp.concatenate([prev, nxt], axis=-1)


def _shift_grid(x):
    b, t, d = x.shape
    g = x.reshape(b, t // GRID_W, GRID_W, d)
    q = d // 4
    left = jnp.pad(g[:, :, :-1, :q], ((0, 0), (0, 0), (1, 0), (0, 0)))
    right = jnp.pad(g[:, :, 1:, q:2 * q], ((0, 0), (0, 0), (0, 1), (0, 0)))
    up = jnp.pad(g[:, :-1, :, 2 * q:3 * q], ((0, 0), (1, 0), (0, 0), (0, 0)))
    down = jnp.pad(g[:, 1:, :, 3 * q:], ((0, 0), (0, 1), (0, 0), (0, 0)))
    return jnp.concatenate([left, right, up, down], axis=-1).reshape(b, t, d)


def _pad_cols(a, cols, value=0.0):
    return jnp.pad(a, ((0, 0), (0, cols - a.shape[1])), constant_values=value)


def kernel(x_prompt, x_sample, state_mlstm_C, state_mlstm_n, state_mlstm_m, state_rwkv_S, c, c_ctx, w_ada, b_ada, g_norm1, g_norm2, w_m_in, b_m_in, g_m_head, w_m_out, mu_rwkv, w_rkv, w_decay0, w_decay1, w_decay2, w_iclr0, w_iclr1, w_iclr2, w_gate1, w_gate2, k_k, k_a, r_k, g_ln_x, b_ln_x, w_rwkv_out, w_ff_gu, w_ff_down, w_router, b_router, w_exp_gu, w_exp_down, g_final):
    nb, seq, d = x_prompt.shape
    db, dseq, _ = x_sample.shape
    n_ctx = nb * seq
    n_lat = db * dseq
    assert d == D_MODEL and n_ctx == GROUP_ROWS and dseq == GROUP_ROWS

    xp = x_prompt.reshape(n_ctx, d)
    xs = x_sample.reshape(n_lat, d)
    cond = jnp.concatenate([c_ctx[None, :], c], axis=0)
    n_groups = cond.shape[0]
    cond = jnp.pad(cond, ((0, (-n_groups) % 8), (0, 0)))
    mod = _ada(cond, w_ada, b_ada).reshape(w_ada.shape[0], cond.shape[0], N_ADA, d)
    row = lambda a: a.reshape(1, -1)

    n_main = 2 * A_QK + 2 * A_V
    w_in = w_m_in[0]
    z, gates = _in_proj(xp, xs, mod, row(g_norm1[0]), w_in[:, :n_main].astype(BF16), row(b_m_in[0][:n_main]),
                        _pad_cols(w_in[:, n_main:], A_GATE_COLS), _pad_cols(row(b_m_in[0][n_main:]), A_GATE_COLS))
    gates_t = gates[:, :4 * A_HEADS].T
    seq_lens = (seq,) * nb + (dseq,) * db
    with_zero_ctx = lambda s: jnp.concatenate([jnp.zeros((nb,) + s.shape[1:], F32), s.astype(F32)], axis=0)
    h_dirs, ctx_state = [], []
    for dirn, rev in ((0, False), (1, True)):
        h_d, cf, nf, mf = _mlstm_scan(
            z, gates, gates_t, with_zero_ctx(state_mlstm_C[:, 0, dirn]), with_zero_ctx(state_mlstm_n[:, 0, dirn]),
            with_zero_ctx(jnp.broadcast_to(state_mlstm_m[:, 0, dirn][..., None], (db, A_HEADS, A_DV))),
            seq_lens=seq_lens, reverse=rev)
        h_dirs.append(h_d)
        ctx_state.append((cf[:nb], nf[:nb], mf[:nb, :, 0]))
    new_c = jnp.stack([s[0] for s in ctx_state], axis=1)[:, None]
    new_n = jnp.stack([s[1] for s in ctx_state], axis=1)[:, None]
    new_m = jnp.stack([s[2] for s in ctx_state], axis=1)[:, None]
    x, hn = _mlstm_out_ff(h_dirs[0], h_dirs[1], z, xp, xs, mod, row(g_m_head[0]), w_m_out[0].astype(BF16),
                          row(g_norm2[0]), w_ff_gu[0].astype(BF16), w_ff_down[0].astype(BF16), row(g_norm1[1]))

    hshift = jnp.concatenate([
        _shift_1d(hn[:n_ctx].reshape(nb, seq, d)).reshape(n_ctx, d),
        _shift_grid(hn[n_ctx:].reshape(db, dseq, d)).reshape(n_lat, d)], axis=0)
    cat_cols = lambda w: jnp.concatenate([w[0], w[1]], axis=1).astype(BF16)
    cat_rows = lambda w: jnp.concatenate([w[0], w[1]], axis=0).astype(BF16)
    r, k, v, lw0, lw1, ia0, ia1, g = _rwkv_in(
        hn, hshift, mu_rwkv[0], w_rkv[0].astype(BF16), cat_cols(w_decay1[0]), cat_rows(w_decay2[0]), w_decay0[0],
        cat_cols(w_iclr1[0]), cat_rows(w_iclr2[0]), w_iclr0[0], w_gate1[0].astype(BF16), w_gate2[0].astype(BF16))
    y_dirs, ctx_s = [], []
    for dirn, rev, lw, ia in ((0, False, lw0, ia0), (1, True, lw1, ia1)):
        y_d, sf = _rwkv_scan(r, k, v, lw, ia, row(k_k[0]), row(k_a[0]), row(r_k[0]),
                             with_zero_ctx(_pair_blockdiag(state_rwkv_S[:, 0, dirn])), seq_lens=seq_lens, reverse=rev)
        y_dirs.append(y_d)
        ctx_s.append(_pair_unblock(sf[:nb]))
    new_s = jnp.stack(ctx_s, axis=1)[:, None]
    x = _rwkv_out(y_dirs[0], y_dirs[1], g, x, mod, row(g_ln_x[0]), row(b_ln_x[0]), w_rwkv_out[0].astype(BF16))
    y_p, y_s = _moe(x, mod, row(g_norm2[1]), _pad_cols(w_router[0], ROUTER_COLS),
                    _pad_cols(row(b_router[0]), ROUTER_COLS, NEG_BIG), w_exp_gu[0].astype(BF16),
                    w_exp_down[0].astype(BF16), row(g_final), n_ctx=n_ctx)
    return (y_p.reshape(nb, seq, d), y_s.reshape(db, dseq, d), new_c, new_n, new_m, new_s)
```

```python
import functools

import jax
import jax.numpy as jnp
from jax import lax
from jax.experimental import pallas as pl
from jax.experimental.pallas import tpu as pltpu

F32 = jnp.float32
BF16 = jnp.bfloat16
HIGHEST = lax.Precision.HIGHEST

D_MODEL = 1024
GROUP_ROWS = 4096
N_ADA = 6
RMS_EPS = 1e-6
GRID_W = 64
A_HEADS = 8
A_DK = 64
A_DV = 128
A_QK = A_HEADS * A_DK
A_V = A_HEADS * A_DV
A_CHUNK = 128
A_GATE_COLS = 128
A_SCAN_ROWS = 256
B_HEADS = 16
B_HEAD_DIM = 64
B_PAIRS = B_HEADS // 2
B_PAIR_W = 2 * B_HEAD_DIM
B_CHUNK = 64
B_DECAY_SCALE = 0.606531
B_LN_EPS = 64e-5
B_SCAN_ROWS = 128
D_FF = 2816
N_EXPERTS = 8
D_FF_EXPERT = D_FF // 2
ROUTER_COLS = 128
NEG_BIG = -1e30

VMEM_LIMIT = 56 * 1024 * 1024


def _cparams(*sem):
    return pltpu.CompilerParams(dimension_semantics=sem, vmem_limit_bytes=VMEM_LIMIT)


def _dot(a, b, precision=None):
    return jnp.dot(a, b, precision=precision, preferred_element_type=F32)


def _dot_nt(a, b, precision=None):
    return lax.dot_general(a, b, (((1,), (1,)), ((), ())), precision=precision, preferred_element_type=F32)


def _dot_tn(a, b, precision=None):
    return lax.dot_general(a, b, (((0,), (0,)), ((), ())), precision=precision, preferred_element_type=F32)


_NN = (((1,), (0,)), ((), ()))
_NT = (((1,), (1,)), ((), ()))
_TN = (((0,), (0,)), ((), ()))


def _bf16_terms(a, n):
    terms = []
    for i in range(n):
        t = a.astype(BF16)
        terms.append(t)
        if i + 1 < n:
            a = a - t.astype(F32)
    return terms


def _mxu(a, b, dims=_NN, na=1, nb=1):
    at = _bf16_terms(a, na)
    bt = _bf16_terms(b, nb)
    out = None
    for i, x in enumerate(at):
        for j, y in enumerate(bt):
            if i + j < max(na, nb):
                t = lax.dot_general(x, y, dims, preferred_element_type=F32)
                out = t if out is None else out + t
    return out


def _norm_mod(x, g, sc, sh):
    y = x * lax.rsqrt(jnp.mean(x * x, axis=-1, keepdims=True) + RMS_EPS)
    return (y * g) * (1.0 + sc) + sh


def _log_sigmoid(x):
    return jnp.minimum(x, 0.0) - jnp.log1p(jnp.exp(-jnp.abs(x)))


def _ada_kernel(cond_ref, w_ref, b_ref, out_ref):
    c = cond_ref[...]
    out_ref[...] = _dot(c * jax.nn.sigmoid(c), w_ref[...], HIGHEST) + b_ref[...]


def _ada(cond_pad, w_ada, b_ada):
    depth, d, cols = w_ada.shape
    rows = cond_pad.shape[0]
    tn = 1536
    return pl.pallas_call(
        _ada_kernel,
        grid=(depth, cols // tn),
        in_specs=[
            pl.BlockSpec((rows, d), lambda l, j: (0, 0)),
            pl.BlockSpec((None, d, tn), lambda l, j: (l, 0, j)),
            pl.BlockSpec((None, 1, tn), lambda l, j: (l, 0, j)),
        ],
        out_specs=pl.BlockSpec((None, rows, tn), lambda l, j: (l, 0, j)),
        out_shape=jax.ShapeDtypeStruct((depth, rows, cols), F32),
        compiler_params=_cparams("parallel", "parallel"),
        name="ada",
    )(cond_pad, w_ada, b_ada.reshape(depth, 1, cols))


def _mlstm_scan_kernel(blk_ref, seq_ref, first_ref, q_ref, k_ref, v_ref, gc_ref, gr_ref, c0_ref, n0_ref, m0_ref,
                       h_ref, cf_ref, nf_ref, mf_ref, c_sc, n_sc, m_sc, *, reverse, n_sub):
    L = A_CHUNK

    @pl.when(first_ref[pl.program_id(0)] == 1)
    def _():
        c_sc[...] = c0_ref[...]
        n_sc[...] = n0_ref[...]
        m_sc[...] = m0_ref[...]

    row = lax.broadcasted_iota(jnp.int32, (L, L), 0)
    col = lax.broadcasted_iota(jnp.int32, (L, L), 1)
    in_win = (col >= row) if reverse else (col <= row)
    in_win_t = (row >= col) if reverse else (row <= col)
    d0 = A_HEADS if reverse else 0

    subs = list(range(n_sub - 1, -1, -1) if reverse else range(n_sub))
    units = [(sub * L, h) for sub in subs for h in range(A_HEADS)]
    fcol = {sub: _log_sigmoid(gc_ref[sub * L:(sub + 1) * L, :]) for sub in subs}
    frow = {sub: _log_sigmoid(gr_ref[:, sub * L:(sub + 1) * L]) for sub in subs}

    ci_l = [d0 + h for _, h in units]
    cf_l = [2 * A_HEADS + d0 + h for _, h in units]
    i_col_l = [gc_ref[r0:r0 + L, ci:ci + 1] for (r0, _), ci in zip(units, ci_l)]
    i_row_l = [gr_ref[ci:ci + 1, r0:r0 + L] for (r0, _), ci in zip(units, ci_l)]
    f_col_l = [fcol[r0 // L][:, cf:cf + 1] for (r0, _), cf in zip(units, cf_l)]
    f_row_l = [frow[r0 // L][cf:cf + 1, :] for (r0, _), cf in zip(units, cf_l)]
    b_col_l = [jnp.sum(jnp.where(in_win, f, 0.0), axis=1, keepdims=True) for f in f_row_l]
    b_row_l = [jnp.sum(jnp.where(in_win_t, f, 0.0), axis=0, keepdims=True) for f in f_col_l]
    tot_l = [jnp.sum(f, axis=1, keepdims=True) for f in f_row_l]
    logw_l = [jnp.where(in_win, bc - br + ir, -jnp.inf) for bc, br, ir in zip(b_col_l, b_row_l, i_row_l)]
    mx_l = [jnp.max(lw, axis=1, keepdims=True) for lw in logw_l]
    p_l = [jnp.exp(lw - mx) for lw, mx in zip(logw_l, mx_l)]
    rep = lambda x: jnp.broadcast_to(x, (x.shape[0], A_DV))
    b_col_l = [rep(b) for b in b_col_l]
    mx_l = [rep(m) for m in mx_l]
    tot_l = [rep(t) for t in tot_l]
    log_g_l = [t - bc + rep(ic) for t, bc, ic in zip(tot_l, b_col_l, i_col_l)]
    mg_l = [jnp.max(lg, axis=0, keepdims=True) for lg in log_g_l]
    wg_l = [jnp.exp(lg - mg)[:, :A_DK] for lg, mg in zip(log_g_l, mg_l)]
    q_l = [q_ref[r0:r0 + L, h * A_DK:(h + 1) * A_DK] for r0, h in units]
    k_l = [k_ref[r0:r0 + L, h * A_DK:(h + 1) * A_DK] * (A_DK ** -0.5) for r0, h in units]
    v_l = [v_ref[r0:r0 + L, h * A_DV:(h + 1) * A_DV].astype(BF16) for r0, h in units]
    qk_l = [_mxu(q, k, _NT) for q, k in zip(q_l, k_l)]
    sp_l = [qk * p for qk, p in zip(qk_l, p_l)]
    ssum_l = [rep(jnp.sum(sp, axis=1, keepdims=True)) for sp in sp_l]
    sv_l = [_mxu(sp, v) for sp, v in zip(sp_l, v_l)]
    kw_l = [k * wg for k, wg in zip(k_l, wg_l)]
    ksum_l = [jnp.sum(kw, axis=0, keepdims=True) for kw in kw_l]
    kv_l = [_mxu(kw, v, _TN) for kw, v in zip(kw_l, v_l)]

    for i0 in range(0, len(units), A_HEADS):
        idx = range(i0, i0 + A_HEADS)
        heads = range(A_HEADS)
        m_prev = [m_sc[h:h + 1, :] for h in heads]
        c_prev = [c_sc[h] for h in heads]
        n_prev = [n_sc[h:h + 1, :] for h in heads]
        qc = [_mxu(q_l[i], c) for i, c in zip(idx, c_prev)]
        qn = [rep(jnp.sum(q_l[i] * n, axis=1, keepdims=True)) for i, n in zip(idx, n_prev)]
        inter = [b_col_l[i] + m for i, m in zip(idx, m_prev)]
        m_row = [jnp.maximum(it, mx_l[i]) for i, it in zip(idx, inter)]
        corr = [jnp.exp(mx_l[i] - mr) for i, mr in zip(idx, m_row)]
        w_inter = [jnp.exp(it - mr) for it, mr in zip(inter, m_row)]
        floor = [jnp.exp(-mr) for mr in m_row]
        num = [cr * sv_l[i] + wi * x for i, cr, wi, x in zip(idx, corr, w_inter, qc)]
        den = [cr * ssum_l[i] + wi * x for i, cr, wi, x in zip(idx, corr, w_inter, qn)]
        m_new = [jnp.maximum(tot_l[i] + m, mg_l[i]) for i, m in zip(idx, m_prev)]
        decay = [jnp.exp(tot_l[i] + m - mn) for i, m, mn in zip(idx, m_prev, m_new)]
        gain = [jnp.exp(mg_l[i] - mn) for i, mn in zip(idx, m_new)]
        for i, h in zip(idx, heads):
            r0 = units[i][0]
            h_ref[r0:r0 + L, h * A_DV:(h + 1) * A_DV] = num[h] / jnp.maximum(jnp.abs(den[h]), floor[h])
            c_sc[h] = decay[h] * c_prev[h] + gain[h] * kv_l[i]
            n_sc[h:h + 1, :] = decay[h][:, :A_DK] * n_prev[h] + gain[h][:, :A_DK] * ksum_l[i]
            m_sc[h:h + 1, :] = m_new[h]

    cf_ref[...] = c_sc[...]
    nf_ref[...] = n_sc[...]
    mf_ref[...] = m_sc[...]


def _scan_schedule(seq_lens, cb, reverse):
    blk, seq, first = [], [], []
    base = 0
    for s, n in enumerate(seq_lens):
        nb = n // cb
        order = range(nb - 1, -1, -1) if reverse else range(nb)
        for j, b in enumerate(order):
            blk.append(base + b)
            seq.append(s)
            first.append(1 if j == 0 else 0)
        base += nb
    as_i32 = lambda a: jnp.asarray(a, dtype=jnp.int32)
    return as_i32(blk), as_i32(seq), as_i32(first)


def _mlstm_scan(z, gates, gates_t, c0, n0, m0, *, seq_lens, reverse):
    cb = A_SCAN_ROWS
    nseq = len(seq_lens)
    rows = sum(seq_lens)
    blk, seq, first = _scan_schedule(seq_lens, cb, reverse)
    tile = lambda cols, col_block: pl.BlockSpec((cb, cols), lambda t, blk, seq, first: (blk[t], col_block))
    state = lambda *shape: pl.BlockSpec((None,) + shape, lambda t, blk, seq, first: (seq[t],) + (0,) * len(shape))
    grid_spec = pltpu.PrefetchScalarGridSpec(
        num_scalar_prefetch=3,
        grid=(blk.shape[0],),
        in_specs=[
            tile(A_QK, 0), tile(A_QK, 1), tile(A_V, 1), tile(A_GATE_COLS, 0),
            pl.BlockSpec((4 * A_HEADS, cb), lambda t, blk, seq, first: (0, blk[t])),
            state(A_HEADS, A_DK, A_DV), state(A_HEADS, A_DK), state(A_HEADS, A_DV),
        ],
        out_specs=[tile(A_V, 0), state(A_HEADS, A_DK, A_DV), state(A_HEADS, A_DK), state(A_HEADS, A_DV)],
        scratch_shapes=[
            pltpu.VMEM((A_HEADS, A_DK, A_DV), F32),
            pltpu.VMEM((A_HEADS, A_DK), F32),
            pltpu.VMEM((A_HEADS, A_DV), F32),
        ],
    )
    kern = functools.partial(_mlstm_scan_kernel, reverse=reverse, n_sub=cb // A_CHUNK)
    return pl.pallas_call(
        kern,
        grid_spec=grid_spec,
        out_shape=[
            jax.ShapeDtypeStruct((rows, A_V), F32),
            jax.ShapeDtypeStruct((nseq, A_HEADS, A_DK, A_DV), F32),
            jax.ShapeDtypeStruct((nseq, A_HEADS, A_DK), F32),
            jax.ShapeDtypeStruct((nseq, A_HEADS, A_DV), F32),
        ],
        compiler_params=_cparams("arbitrary"),
        name="mlstm_scan_bwd" if reverse else "mlstm_scan_fwd",
    )(blk, seq, first, z, z, z, gates, gates_t, c0, n0, m0)


def _rwkv_scan_kernel(blk_ref, seq_ref, first_ref, r_ref, k_ref, v_ref, lw_ref, a_ref, kk_ref, ka_ref, rk_ref,
                      s0_ref, y_ref, sf_ref, s_sc, *, reverse, n_sub):
    L = B_CHUNK
    W = B_PAIR_W
    R = 2 * L

    @pl.when(first_ref[pl.program_id(0)] == 1)
    def _():
        s_sc[...] = s0_ref[...]

    lane = lax.broadcasted_iota(jnp.int32, (L, W), 1)
    head0 = lane < B_HEAD_DIM
    ri = lax.broadcasted_iota(jnp.int32, (R, R), 0)
    ci = lax.broadcasted_iota(jnp.int32, (R, R), 1)
    same_head = (ri >= L) == (ci >= L)
    strict = same_head & ((ri < ci) if reverse else (ri > ci))
    incl = same_head & ((ri <= ci) if reverse else (ri >= ci))
    eye = (ri == ci).astype(F32)
    wi = lax.broadcasted_iota(jnp.int32, (W, W), 0)
    wj = lax.broadcasted_iota(jnp.int32, (W, W), 1)
    bd = (wi >= B_HEAD_DIM) == (wj >= B_HEAD_DIM)
    bd_ones = jnp.where(bd, 1.0, 0.0).astype(BF16)
    ti = lax.broadcasted_iota(jnp.int32, (L, L), 0)
    tj = lax.broadcasted_iota(jnp.int32, (L, L), 1)
    tri = jnp.where((tj >= ti) if reverse else (tj <= ti), 1.0, 0.0).astype(BF16)

    def stack_masked(x):
        return jnp.concatenate([jnp.where(head0, x, 0.0), jnp.where(head0, 0.0, x)], axis=0)

    def stack_dup(x):
        return jnp.concatenate([x, x], axis=0)

    subs = list(range(n_sub - 1, -1, -1) if reverse else range(n_sub))
    units = [(sub * L, slice(p * W, (p + 1) * W)) for sub in subs for p in range(B_PAIRS)]
    nu = len(units)

    def each(fn, *lists):
        return [fn(*args) for args in zip(*lists)]

    r_l = [r_ref[r0:r0 + L, sl] for r0, sl in units]
    k_l = [k_ref[r0:r0 + L, sl] for r0, sl in units]
    v_l = [v_ref[r0:r0 + L, sl] for r0, sl in units]
    lw_l = [-B_DECAY_SCALE * jax.nn.sigmoid(lw_ref[r0:r0 + L, sl]) for r0, sl in units]
    a_l = [jax.nn.sigmoid(a_ref[r0:r0 + L, sl]) for r0, sl in units]
    kkr_l = [k * kk_ref[:, sl] for k, (_, sl) in zip(k_l, units)]
    kd_l = [k * (1.0 + (a - 1.0) * ka_ref[:, sl]) for k, a, (_, sl) in zip(k_l, a_l, units)]
    ss_l = [_mxu(kkr * kkr, bd_ones, na=2) for kkr in kkr_l]
    bsum_l = [_mxu(r * kd * rk_ref[:, sl], bd_ones, na=2) for r, kd, (_, sl) in zip(r_l, kd_l, units)]
    c_inc_l = [_mxu(tri, lw, nb=2) for lw in lw_l]
    kk_l = each(lambda kkr, ss: kkr / jnp.maximum(jnp.sqrt(ss), 1e-12), kkr_l, ss_l)
    bv_l = each(lambda kk, a: kk * a, kk_l, a_l)
    bonus_l = each(lambda bs, v: bs * v, bsum_l, v_l)
    c_mid_l = [c[L // 2:L // 2 + 1, :] for c in c_inc_l]
    tot_l = [(c[0:1, :] if reverse else c[L - 1:L, :]) for c in c_inc_l]
    x_all_l = each(lambda kk, r, c, lw: jnp.concatenate(
        [stack_masked(-kk * jnp.exp(c - lw)), stack_masked(r * jnp.exp(c))], axis=0), kk_l, r_l, c_inc_l, lw_l)
    xa_l = each(lambda x, cm: (x * jnp.exp(-cm)).astype(BF16), x_all_l, c_mid_l)
    x_all_l = [x.astype(BF16) for x in x_all_l]

    def decayed(bv, kd, e):
        return jnp.concatenate([stack_dup(bv * e), stack_dup(kd * e)], axis=0).astype(BF16)

    y_all_l = each(lambda bv, kd, c, cm: decayed(bv, kd, jnp.exp(cm - c)), bv_l, kd_l, c_inc_l, c_mid_l)
    y_end_l = each(lambda bv, kd, c, t: decayed(bv, kd, jnp.exp(t - c)), bv_l, kd_l, c_inc_l, tot_l)
    v_s_l = [stack_masked(v).astype(BF16) for v in v_l]

    a_all_l = each(lambda x, y: _mxu(x, y, _NT), xa_l, y_all_l)
    a_ab_l = [jnp.where(strict, a[:R, :R], 0.0) for a in a_all_l]
    a_ak_l = [jnp.where(strict, a[:R, R:], 0.0).astype(BF16) for a in a_all_l]
    a_r_l = [jnp.concatenate([jnp.where(incl, a[R:, :R], 0.0), jnp.where(incl, a[R:, R:], 0.0)],
                             axis=1).astype(BF16) for a in a_all_l]
    akv_l = each(lambda a, v: _mxu(a, v, _NN), a_ak_l, v_s_l)

    t_inv_l = [eye + a for a in a_ab_l]
    pw_l = [_mxu(a, a, _NN) for a in a_ab_l]
    for _ in range(4):
        both_l = each(lambda pw, t: _mxu(pw, jnp.concatenate([pw, t], axis=1), _NN), pw_l, t_inv_l)
        pw_l = [b[:, :R] for b in both_l]
        t_inv_l = each(lambda t, b: t + b[:, R:], t_inv_l, both_l)
    t_inv_l = each(lambda t, pw: (t + _mxu(pw, t, _NN)).astype(BF16), t_inv_l, pw_l)

    for i0 in range(0, nu, B_PAIRS):
        idx = list(range(i0, i0 + B_PAIRS))
        s_prev_l = [s_sc[p] for p in range(B_PAIRS)]
        xs_l = [_mxu(x_all_l[i], s, _NT) for i, s in zip(idx, s_prev_l)]
        u_l = [_mxu(t_inv_l[i], xs[:R] + akv_l[i], _NN) for i, xs in zip(idx, xs_l)]
        uv_l = [jnp.concatenate([u.astype(BF16), v_s_l[i]], axis=0) for i, u in zip(idx, u_l)]
        y_s_l = [xs[R:] + _mxu(a_r_l[i], uv, _NN) for i, xs, uv in zip(idx, xs_l, uv_l)]
        for i, y_s in zip(idx, y_s_l):
            r0, sl = units[i]
            y_ref[r0:r0 + L, sl] = y_s[:L] + y_s[L:] + bonus_l[i]
        s_new_l = [s * jnp.exp(tot_l[i]) + _mxu(uv, y_end_l[i], _TN)
                   for i, s, uv in zip(idx, s_prev_l, uv_l)]
        for p, s_new in enumerate(s_new_l):
            s_sc[p] = jnp.where(bd, s_new, 0.0)

    sf_ref[...] = s_sc[...]


def _rwkv_scan(r, k, v, lw, a, k_k, k_a, r_k, s0, *, seq_lens, reverse):
    cb = B_SCAN_ROWS
    nseq = len(seq_lens)
    rows = sum(seq_lens)
    blk, seq, first = _scan_schedule(seq_lens, cb, reverse)
    tile = pl.BlockSpec((cb, D_MODEL), lambda t, blk, seq, first: (blk[t], 0))
    vec = pl.BlockSpec((1, D_MODEL), lambda t, blk, seq, first: (0, 0))
    st = pl.BlockSpec((None, B_PAIRS, B_PAIR_W, B_PAIR_W), lambda t, blk, seq, first: (seq[t], 0, 0, 0))
    grid_spec = pltpu.PrefetchScalarGridSpec(
        num_scalar_prefetch=3,
        grid=(blk.shape[0],),
        in_specs=[tile, tile, tile, tile, tile, vec, vec, vec, st],
        out_specs=[tile, st],
        scratch_shapes=[pltpu.VMEM((B_PAIRS, B_PAIR_W, B_PAIR_W), F32)],
    )
    kern = functools.partial(_rwkv_scan_kernel, reverse=reverse, n_sub=cb // B_CHUNK)
    return pl.pallas_call(
        kern,
        grid_spec=grid_spec,
        out_shape=[
            jax.ShapeDtypeStruct((rows, D_MODEL), F32),
            jax.ShapeDtypeStruct((nseq, B_PAIRS, B_PAIR_W, B_PAIR_W), F32),
        ],
        compiler_params=_cparams("arbitrary"),
        name="rwkv_scan_bwd" if reverse else "rwkv_scan_fwd",
    )(blk, seq, first, r, k, v, lw, a, k_k, k_a, r_k, s0)


def _pair_blockdiag(s):
    n = s.shape[0]
    s = s.reshape(n, B_PAIRS, 2, B_HEAD_DIM, B_HEAD_DIM)
    z = jnp.zeros_like(s[:, :, 0])
    top = jnp.concatenate([s[:, :, 0], z], axis=-1)
    bot = jnp.concatenate([z, s[:, :, 1]], axis=-1)
    return jnp.concatenate([top, bot], axis=-2)


def _pair_unblock(sp):
    n = sp.shape[0]
    h0 = sp[:, :, :B_HEAD_DIM, :B_HEAD_DIM]
    h1 = sp[:, :, B_HEAD_DIM:, B_HEAD_DIM:]
    return jnp.stack([h0, h1], axis=2).reshape(n, B_HEADS, B_HEAD_DIM, B_HEAD_DIM)


TM = 256
TM_WIDE = 512
TM_MOE = 1024
MOE_SUB = 512
MOE_CAP = 160


def _mod_spec(layer, tm):
    per_group = GROUP_ROWS // tm
    return pl.BlockSpec((None, None, N_ADA, D_MODEL), lambda i, *_: (layer, i // per_group, 0, 0))


def _row_spec(tm, cols, col_block=0):
    return pl.BlockSpec((tm, cols), lambda i, *_: (i, col_block))


def _full_spec(shape):
    nd = len(shape)
    return pl.BlockSpec(shape, lambda i, *_: (0,) * nd)


def _split_specs(tm, cols, n_first):
    t_first = n_first // tm
    return (pl.BlockSpec((tm, cols), lambda i, *_: (jnp.minimum(i, t_first - 1), 0)),
            pl.BlockSpec((tm, cols), lambda i, *_: (jnp.maximum(i - t_first, 0), 0)))


def _in_proj_kernel(xp_ref, xs_ref, mod_ref, g_ref, w_ref, b_ref, wg_ref, bg_ref, z_ref, gate_ref, *, ctx_tiles):
    x = jnp.where(pl.program_id(0) < ctx_tiles, xp_ref[...], xs_ref[...])
    hn = _norm_mod(x, g_ref[...], mod_ref[1:2, :], mod_ref[0:1, :])
    z_ref[...] = _dot(hn.astype(BF16), w_ref[...]) + b_ref[...]
    gate_ref[...] = _mxu(hn, wg_ref[...], na=2, nb=2) + bg_ref[...]


def _in_proj(xp, xs, mod, g1, w_main, b_main, w_gate, b_gate):
    rows = xp.shape[0] + xs.shape[0]
    cols = w_main.shape[1]
    tm = TM_WIDE
    return pl.pallas_call(
        functools.partial(_in_proj_kernel, ctx_tiles=xp.shape[0] // tm),
        grid=(rows // tm,),
        in_specs=[
            *_split_specs(tm, D_MODEL, xp.shape[0]), _mod_spec(0, tm), _full_spec((1, D_MODEL)),
            _full_spec(w_main.shape), _full_spec(b_main.shape), _full_spec(w_gate.shape), _full_spec(b_gate.shape),
        ],
        out_specs=[_row_spec(tm, cols), _row_spec(tm, A_GATE_COLS)],
        out_shape=[jax.ShapeDtypeStruct((rows, cols), F32), jax.ShapeDtypeStruct((rows, A_GATE_COLS), F32)],
        compiler_params=_cparams("parallel"),
        name="mlstm_in_proj",
    )(xp, xs, mod, g1, w_main, b_main, w_gate, b_gate)


def _mlstm_out_ff_kernel(hf_ref, hb_ref, o_ref, xp_ref, xs_ref, mod_ref, modn_ref, gh_ref, wo_ref, g2_ref,
                         wgu_ref, wdn_ref, gn_ref, x_out_ref, hn_out_ref, *, ctx_tiles):
    x = jnp.where(pl.program_id(0) < ctx_tiles, xp_ref[...], xs_ref[...])
    hs = hf_ref[...] + hb_ref[...]
    parts = []
    for h in range(A_HEADS):
        seg = hs[:, h * A_DV:(h + 1) * A_DV]
        parts.append(seg * lax.rsqrt(jnp.mean(seg * seg, axis=-1, keepdims=True) + RMS_EPS))
    y = (jnp.concatenate(parts, axis=1) * gh_ref[...]) * jax.nn.sigmoid(o_ref[...])
    x1 = x + mod_ref[2:3, :] * _dot(y.astype(BF16), wo_ref[...])
    hn2 = _norm_mod(x1, g2_ref[...], mod_ref[4:5, :], mod_ref[3:4, :])
    gu = _dot(hn2.astype(BF16), wgu_ref[...])
    act = jax.nn.silu(gu[:, :D_FF]) * gu[:, D_FF:]
    x2 = x1 + mod_ref[5:6, :] * _dot(act.astype(BF16), wdn_ref[...])
    x_out_ref[...] = x2
    hn_out_ref[...] = _norm_mod(x2, gn_ref[...], modn_ref[1:2, :], modn_ref[0:1, :])


def _mlstm_out_ff(hf, hb, z, xp, xs, mod, g_head, w_out, g2, w_gu, w_down, g1_next):
    rows = xp.shape[0] + xs.shape[0]
    return pl.pallas_call(
        functools.partial(_mlstm_out_ff_kernel, ctx_tiles=xp.shape[0] // TM),
        grid=(rows // TM,),
        in_specs=[
            _row_spec(TM, A_V), _row_spec(TM, A_V), _row_spec(TM, A_V, 2), *_split_specs(TM, D_MODEL, xp.shape[0]),
            _mod_spec(0, TM), _mod_spec(1, TM), _full_spec((1, A_V)), _full_spec(w_out.shape),
            _full_spec((1, D_MODEL)), _full_spec(w_gu.shape), _full_spec(w_down.shape), _full_spec((1, D_MODEL)),
        ],
        out_specs=[_row_spec(TM, D_MODEL), _row_spec(TM, D_MODEL)],
        out_shape=[jax.ShapeDtypeStruct((rows, D_MODEL), F32)] * 2,
        compiler_params=_cparams("parallel"),
        name="mlstm_out_ff",
    )(hf, hb, z, xp, xs, mod, mod, g_head, w_out, g2, w_gu, w_down, g1_next)


def _rwkv_in_kernel(hn_ref, hs_ref, mu_ref, wrkv_ref, w1_ref, w2_ref, w0_ref, a1_ref, a2_ref, a0_ref,
                    g1_ref, g2_ref, r_ref, k_ref, v_ref, lw0_ref, lw1_ref, ia0_ref, ia1_ref, g_ref):
    h = hn_ref[...]
    xx = hs_ref[...] - h

    def mix(c):
        return (h + xx * mu_ref[c:c + 1, :]).astype(BF16)

    r_ref[...] = _dot(mix(0), wrkv_ref[0])
    k_ref[...] = _dot(mix(1), wrkv_ref[1])
    v_ref[...] = _dot(mix(2), wrkv_ref[2])
    lane = lax.broadcasted_iota(jnp.int32, (h.shape[0], 2 * 64), 1)
    dir0 = lane < 64
    t = jnp.tanh(_dot(mix(3), w1_ref[...]))
    lw0_ref[...] = w0_ref[0:1, :] + _dot(jnp.where(dir0, t, 0.0).astype(BF16), w2_ref[...])
    lw1_ref[...] = w0_ref[1:2, :] + _dot(jnp.where(dir0, 0.0, t).astype(BF16), w2_ref[...])
    u = _dot(mix(4), a1_ref[...])
    ia0_ref[...] = a0_ref[0:1, :] + _dot(jnp.where(dir0, u, 0.0).astype(BF16), a2_ref[...])
    ia1_ref[...] = a0_ref[1:2, :] + _dot(jnp.where(dir0, 0.0, u).astype(BF16), a2_ref[...])
    g_ref[...] = _dot(jax.nn.sigmoid(_dot(mix(5), g1_ref[...])).astype(BF16), g2_ref[...])


def _rwkv_in(hn, hshift, mu, w_rkv, w1, w2, w0, a1, a2, a0, g1, g2):
    rows = hn.shape[0]
    ins = [mu, w_rkv, w1, w2, w0, a1, a2, a0, g1, g2]
    return pl.pallas_call(
        _rwkv_in_kernel,
        grid=(rows // TM,),
        in_specs=[_row_spec(TM, D_MODEL), _row_spec(TM, D_MODEL)] + [_full_spec(a.shape) for a in ins],
        out_specs=[_row_spec(TM, D_MODEL)] * 8,
        out_shape=[jax.ShapeDtypeStruct((rows, D_MODEL), F32)] * 8,
        compiler_params=_cparams("parallel"),
        name="rwkv_in_proj",
    )(hn, hshift, *ins)


def _rwkv_out_kernel(yf_ref, yb_ref, g_ref, x_ref, mod_ref, lng_ref, lnb_ref, wo_ref, x_out_ref):
    y = yf_ref[...] + yb_ref[...]
    wi = lax.broadcasted_iota(jnp.int32, (B_PAIR_W, B_PAIR_W), 0)
    wj = lax.broadcasted_iota(jnp.int32, (B_PAIR_W, B_PAIR_W), 1)
    bd_mean = jnp.where((wi >= B_HEAD_DIM) == (wj >= B_HEAD_DIM), 1.0 / B_HEAD_DIM, 0.0)
    parts = []
    for p in range(B_PAIRS):
        seg = y[:, p * B_PAIR_W:(p + 1) * B_PAIR_W]
        cen = seg - _mxu(seg, bd_mean, na=3)
        var = _mxu(cen * cen, bd_mean, na=3)
        parts.append(cen * lax.rsqrt(var + B_LN_EPS))
    yn = jnp.concatenate(parts, axis=1) * lng_ref[...] + lnb_ref[...]
    out = _dot((yn * g_ref[...]).astype(BF16), wo_ref[...])
    x_out_ref[...] = x_ref[...] + mod_ref[2:3, :] * out


def _rwkv_out(yf, yb, g, x, mod, ln_g, ln_b, w_out):
    rows = x.shape[0]
    tm = TM_WIDE
    return pl.pallas_call(
        _rwkv_out_kernel,
        grid=(rows // tm,),
        in_specs=[_row_spec(tm, D_MODEL)] * 4 + [
            _mod_spec(1, tm), _full_spec((1, D_MODEL)), _full_spec((1, D_MODEL)), _full_spec(w_out.shape)],
        out_specs=_row_spec(tm, D_MODEL),
        out_shape=jax.ShapeDtypeStruct((rows, D_MODEL), F32),
        compiler_params=_cparams("parallel"),
        name="rwkv_out_proj",
    )(yf, yb, g, x, mod, ln_g, ln_b, w_out)


def _moe_kernel(x_ref, mod_ref, g2_ref, wr_ref, br_ref, upper_ref, wgu_ref, wdn_ref, gf_ref, out_ref,
                hn_sc, sel_sc, cw_sc, rank_sc, acc_sc):
    e = pl.program_id(1)
    tm = x_ref.shape[0]
    n_sub = tm // MOE_SUB

    @pl.when(e == 0)
    def _():
        hn = _norm_mod(x_ref[...], g2_ref[...], mod_ref[4:5, :], mod_ref[3:4, :])
        hn_sc[...] = hn.astype(BF16)
        logits = _mxu(hn, wr_ref[...], na=3, nb=3) + br_ref[...]
        lane = lax.broadcasted_iota(jnp.int32, logits.shape, 1)
        m1 = jnp.max(logits, axis=1, keepdims=True)
        i1 = jnp.min(jnp.where(logits == m1, lane, ROUTER_COLS), axis=1, keepdims=True)
        rest = jnp.where(lane == i1, -jnp.inf, logits)
        m2 = jnp.max(rest, axis=1, keepdims=True)
        i2 = jnp.min(jnp.where(rest == m2, lane, ROUTER_COLS), axis=1, keepdims=True)
        e2 = jnp.exp(m2 - m1)
        den = 1.0 + e2
        comb = jnp.where(lane == i1, 1.0 / den, 0.0) + jnp.where(lane == i2, e2 / den, 0.0)
        sel = jnp.where((lane == i1) | (lane == i2), 1.0, 0.0)
        for s in range(n_sub):
            rows = slice(s * MOE_SUB, (s + 1) * MOE_SUB)
            sel_t = sel[rows, :].T[:N_EXPERTS, :]
            sel_sc[s] = sel_t
            cw_sc[s] = comb[rows, :].T[:N_EXPERTS, :]
            rank_sc[s] = _mxu(sel_t, upper_ref[...]).astype(jnp.int32)
        acc_sc[...] = jnp.zeros_like(acc_sc)

    slot = lax.broadcasted_iota(jnp.int32, (MOE_CAP, MOE_SUB), 0)
    for s in range(n_sub):
        rows = slice(s * MOE_SUB, (s + 1) * MOE_SUB)
        sel_row = sel_sc[s, pl.ds(e, 1), :]
        cw_row = cw_sc[s, pl.ds(e, 1), :]
        rank_row = rank_sc[s, pl.ds(e, 1), :]
        count = jnp.sum(sel_row).astype(jnp.int32)

        def chunk(c, carry):
            pick = jnp.where((rank_row == slot + c * MOE_CAP) & (sel_row > 0.5), 1.0, 0.0)
            pick_bf = pick.astype(BF16)
            xg = _mxu(pick_bf, hn_sc[rows, :])
            gu = _mxu(xg, wgu_ref[...])
            act = jax.nn.silu(gu[:, :D_FF_EXPERT]) * gu[:, D_FF_EXPERT:]
            ye = _mxu(act, wdn_ref[...])
            wj = jnp.sum(pick * cw_row, axis=1, keepdims=True)
            acc_sc[rows, :] += _mxu(pick_bf, wj * ye, _TN)
            return carry

        lax.fori_loop(0, (count + MOE_CAP - 1) // MOE_CAP, chunk, 0)

    @pl.when(e == N_EXPERTS - 1)
    def _():
        x4 = x_ref[...] + mod_ref[5:6, :] * acc_sc[...]
        out_ref[...] = (x4 * lax.rsqrt(jnp.mean(x4 * x4, axis=-1, keepdims=True) + RMS_EPS)) * gf_ref[...]


def _moe(x, mod, g2, w_router, b_router, w_gu, w_down, g_final):
    rows = x.shape[0]
    tm = TM_MOE
    n_sub = tm // MOE_SUB
    upper = jnp.triu(jnp.ones((MOE_SUB, MOE_SUB), BF16), k=1)
    return pl.pallas_call(
        _moe_kernel,
        grid=(rows // tm, N_EXPERTS),
        in_specs=[
            _row_spec(tm, D_MODEL), _mod_spec(1, tm), _full_spec((1, D_MODEL)),
            _full_spec(w_router.shape), _full_spec(b_router.shape), _full_spec(upper.shape),
            pl.BlockSpec((None, D_MODEL, 2 * D_FF_EXPERT), lambda i, e: (e, 0, 0)),
            pl.BlockSpec((None, D_FF_EXPERT, D_MODEL), lambda i, e: (e, 0, 0)),
            _full_spec((1, D_MODEL)),
        ],
        out_specs=_row_spec(tm, D_MODEL),
        out_shape=jax.ShapeDtypeStruct((rows, D_MODEL), F32),
        scratch_shapes=[
            pltpu.VMEM((tm, D_MODEL), BF16),
            pltpu.VMEM((n_sub, N_EXPERTS, MOE_SUB), F32),
            pltpu.VMEM((n_sub, N_EXPERTS, MOE_SUB), F32),
            pltpu.VMEM((n_sub, N_EXPERTS, MOE_SUB), jnp.int32),
            pltpu.VMEM((tm, D_MODEL), F32),
        ],
        compiler_params=_cparams("parallel", "arbitrary"),
        name="moe_final",
    )(x, mod, g2, w_router, b_router, upper, w_gu, w_down, g_final)


def _shift_1d(x):
    half = x.shape[-1] // 2
    prev = jnp.pad(x[:, :-1, :half], ((0, 0), (1, 0), (0, 0)))
    nxt = jnp.pad(x[:, 1:, half:], ((0, 0), (0, 1), (0, 0)))
    return jnp.concatenate([prev, nxt], axis=-1)


def _shift_grid(x):
    b, t, d = x.shape
    g = x.reshape(b, t // GRID_W, GRID_W, d)
    q = d // 4
    left = jnp.pad(g[:, :, :-1, :q], ((0, 0), (0, 0), (1, 0), (0, 0)))
    right = jnp.pad(g[:, :, 1:, q:2 * q], ((0, 0), (0, 0), (0, 1), (0, 0)))
    up = jnp.pad(g[:, :-1, :, 2 * q:3 * q], ((0, 0), (1, 0), (0, 0), (0, 0)))
    down = jnp.pad(g[:, 1:, :, 3 * q:], ((0, 0), (0, 1), (0, 0), (0, 0)))
    return jnp.concatenate([left, right, up, down], axis=-1).reshape(b, t, d)


def _pad_cols(a, cols, value=0.0):
    return jnp.pad(a, ((0, 0), (0, cols - a.shape[1])), constant_values=value)


def kernel(x_prompt, x_sample, state_mlstm_C, state_mlstm_n, state_mlstm_m, state_rwkv_S, c, c_ctx, w_ada, b_ada, g_norm1, g_norm2, w_m_in, b_m_in, g_m_head, w_m_out, mu_rwkv, w_rkv, w_decay0, w_decay1, w_decay2, w_iclr0, w_iclr1, w_iclr2, w_gate1, w_gate2, k_k, k_a, r_k, g_ln_x, b_ln_x, w_rwkv_out, w_ff_gu, w_ff_down, w_router, b_router, w_exp_gu, w_exp_down, g_final):
    nb, seq, d = x_prompt.shape
    db, dseq, _ = x_sample.shape
    n_ctx = nb * seq
    n_lat = db * dseq
    assert d == D_MODEL and n_ctx == GROUP_ROWS and dseq == GROUP_ROWS

    xp = x_prompt.reshape(n_ctx, d)
    xs = x_sample.reshape(n_lat, d)
    cond = jnp.concatenate([c_ctx[None, :], c], axis=0)
    n_groups = cond.shape[0]
    cond = jnp.pad(cond, ((0, (-n_groups) % 8), (0, 0)))
    mod = _ada(cond, w_ada, b_ada).reshape(w_ada.shape[0], cond.shape[0], N_ADA, d)
    row = lambda a: a.reshape(1, -1)

    n_main = 2 * A_QK + 2 * A_V
    w_in = w_m_in[0]
    z, gates = _in_proj(xp, xs, mod, row(g_norm1[0]), w_in[:, :n_main].astype(BF16), row(b_m_in[0][:n_main]),
                        _pad_cols(w_in[:, n_main:], A_GATE_COLS), _pad_cols(row(b_m_in[0][n_main:]), A_GATE_COLS))
    gates_t = gates[:, :4 * A_HEADS].T
    seq_lens = (seq,) * nb + (dseq,) * db
    with_zero_ctx = lambda s: jnp.concatenate([jnp.zeros((nb,) + s.shape[1:], F32), s.astype(F32)], axis=0)
    h_dirs, ctx_state = [], []
    for dirn, rev in ((0, False), (1, True)):
        h_d, cf, nf, mf = _mlstm_scan(
            z, gates, gates_t, with_zero_ctx(state_mlstm_C[:, 0, dirn]), with_zero_ctx(state_mlstm_n[:, 0, dirn]),
            with_zero_ctx(jnp.broadcast_to(state_mlstm_m[:, 0, dirn][..., None], (db, A_HEADS, A_DV))),
            seq_lens=seq_lens, reverse=rev)
        h_dirs.append(h_d)
        ctx_state.append((cf[:nb], nf[:nb], mf[:nb, :, 0]))
    new_c = jnp.stack([s[0] for s in ctx_state], axis=1)[:, None]
    new_n = jnp.stack([s[1] for s in ctx_state], axis=1)[:, None]
    new_m = jnp.stack([s[2] for s in ctx_state], axis=1)[:, None]
    x, hn = _mlstm_out_ff(h_dirs[0], h_dirs[1], z, xp, xs, mod, row(g_m_head[0]), w_m_out[0].astype(BF16),
                          row(g_norm2[0]), w_ff_gu[0].astype(BF16), w_ff_down[0].astype(BF16), row(g_norm1[1]))

    hshift = jnp.concatenate([
        _shift_1d(hn[:n_ctx].reshape(nb, seq, d)).reshape(n_ctx, d),
        _shift_grid(hn[n_ctx:].reshape(db, dseq, d)).reshape(n_lat, d)], axis=0)
    cat_cols = lambda w: jnp.concatenate([w[0], w[1]], axis=1).astype(BF16)
    cat_rows = lambda w: jnp.concatenate([w[0], w[1]], axis=0).astype(BF16)
    r, k, v, lw0, lw1, ia0, ia1, g = _rwkv_in(
        hn, hshift, mu_rwkv[0], w_rkv[0].astype(BF16), cat_cols(w_decay1[0]), cat_rows(w_decay2[0]), w_decay0[0],
        cat_cols(w_iclr1[0]), cat_rows(w_iclr2[0]), w_iclr0[0], w_gate1[0].astype(BF16), w_gate2[0].astype(BF16))
    y_dirs, ctx_s = [], []
    for dirn, rev, lw, ia in ((0, False, lw0, ia0), (1, True, lw1, ia1)):
        y_d, sf = _rwkv_scan(r, k, v, lw, ia, row(k_k[0]), row(k_a[0]), row(r_k[0]),
                             with_zero_ctx(_pair_blockdiag(state_rwkv_S[:, 0, dirn])), seq_lens=seq_lens, reverse=rev)
        y_dirs.append(y_d)
        ctx_s.append(_pair_unblock(sf[:nb]))
    new_s = jnp.stack(ctx_s, axis=1)[:, None]
    x = _rwkv_out(y_dirs[0], y_dirs[1], g, x, mod, row(g_ln_x[0]), row(b_ln_x[0]), w_rwkv_out[0].astype(BF16))
    y = _moe(x, mod, row(g_norm2[1]), _pad_cols(w_router[0], ROUTER_COLS),
             _pad_cols(row(b_router[0]), ROUTER_COLS, NEG_BIG), w_exp_gu[0].astype(BF16),
             w_exp_down[0].astype(BF16), row(g_final))
    return (y[:n_ctx].reshape(nb, seq, d), y[n_ctx:].reshape(db, dseq, d), new_c, new_n, new_m, new_s)
```

```python
import functools

import jax
import jax.numpy as jnp
from jax import lax
from jax.experimental import pallas as pl
from jax.experimental.pallas import tpu as pltpu

F32 = jnp.float32
BF16 = jnp.bfloat16
HIGHEST = lax.Precision.HIGHEST

D_MODEL = 1024
GROUP_ROWS = 4096
N_ADA = 6
RMS_EPS = 1e-6
GRID_W = 64
A_HEADS = 8
A_DK = 64
A_DV = 128
A_QK = A_HEADS * A_DK
A_V = A_HEADS * A_DV
A_CHUNK = 128
A_GATE_COLS = 128
A_SCAN_ROWS = 256
B_HEADS = 16
B_HEAD_DIM = 64
B_PAIRS = B_HEADS // 2
B_PAIR_W = 2 * B_HEAD_DIM
B_CHUNK = 64
B_DECAY_SCALE = 0.606531
B_LN_EPS = 64e-5
B_SCAN_ROWS = 128
D_FF = 2816
N_EXPERTS = 8
D_FF_EXPERT = D_FF // 2
ROUTER_COLS = 128
NEG_BIG = -1e30

VMEM_LIMIT = 56 * 1024 * 1024


def _cparams(*sem):
    return pltpu.CompilerParams(dimension_semantics=sem, vmem_limit_bytes=VMEM_LIMIT)


def _dot(a, b, precision=None):
    return jnp.dot(a, b, precision=precision, preferred_element_type=F32)


def _dot_nt(a, b, precision=None):
    return lax.dot_general(a, b, (((1,), (1,)), ((), ())), precision=precision, preferred_element_type=F32)


def _dot_tn(a, b, precision=None):
    return lax.dot_general(a, b, (((0,), (0,)), ((), ())), precision=precision, preferred_element_type=F32)


_NN = (((1,), (0,)), ((), ()))
_NT = (((1,), (1,)), ((), ()))
_TN = (((0,), (0,)), ((), ()))


def _bf16_terms(a, n):
    terms = []
    for i in range(n):
        t = a.astype(BF16)
        terms.append(t)
        if i + 1 < n:
            a = a - t.astype(F32)
    return terms


def _mxu(a, b, dims=_NN, na=1, nb=1):
    at = _bf16_terms(a, na)
    bt = _bf16_terms(b, nb)
    out = None
    for i, x in enumerate(at):
        for j, y in enumerate(bt):
            if i + j < max(na, nb):
                t = lax.dot_general(x, y, dims, preferred_element_type=F32)
                out = t if out is None else out + t
    return out


def _norm_mod(x, g, sc, sh):
    y = x * lax.rsqrt(jnp.mean(x * x, axis=-1, keepdims=True) + RMS_EPS)
    return (y * g) * (1.0 + sc) + sh


def _log_sigmoid(x):
    return jnp.minimum(x, 0.0) - jnp.log1p(jnp.exp(-jnp.abs(x)))


def _ada_kernel(cond_ref, w_ref, b_ref, out_ref):
    c = cond_ref[...]
    out_ref[...] = _dot(c * jax.nn.sigmoid(c), w_ref[...], HIGHEST) + b_ref[...]


def _ada(cond_pad, w_ada, b_ada):
    depth, d, cols = w_ada.shape
    rows = cond_pad.shape[0]
    tn = 1536
    return pl.pallas_call(
        _ada_kernel,
        grid=(depth, cols // tn),
        in_specs=[
            pl.BlockSpec((rows, d), lambda l, j: (0, 0)),
            pl.BlockSpec((None, d, tn), lambda l, j: (l, 0, j)),
            pl.BlockSpec((None, 1, tn), lambda l, j: (l, 0, j)),
        ],
        out_specs=pl.BlockSpec((None, rows, tn), lambda l, j: (l, 0, j)),
        out_shape=jax.ShapeDtypeStruct((depth, rows, cols), F32),
        compiler_params=_cparams("parallel", "parallel"),
        name="ada",
    )(cond_pad, w_ada, b_ada.reshape(depth, 1, cols))


def _mlstm_scan_kernel(blk_ref, seq_ref, first_ref, q_ref, k_ref, v_ref, gc_ref, gr_ref, c0_ref, n0_ref, m0_ref,
                       h_ref, cf_ref, nf_ref, mf_ref, c_sc, n_sc, m_sc, *, reverse, n_sub):
    L = A_CHUNK

    @pl.when(first_ref[pl.program_id(0)] == 1)
    def _():
        c_sc[...] = c0_ref[...]
        n_sc[...] = n0_ref[...]
        m_sc[...] = m0_ref[...]

    row = lax.broadcasted_iota(jnp.int32, (L, L), 0)
    col = lax.broadcasted_iota(jnp.int32, (L, L), 1)
    in_win = (col >= row) if reverse else (col <= row)
    in_win_t = (row >= col) if reverse else (row <= col)
    d0 = A_HEADS if reverse else 0

    subs = list(range(n_sub - 1, -1, -1) if reverse else range(n_sub))
    units = [(sub * L, h) for sub in subs for h in range(A_HEADS)]
    fcol = {sub: _log_sigmoid(gc_ref[sub * L:(sub + 1) * L, :]) for sub in subs}
    frow = {sub: _log_sigmoid(gr_ref[:, sub * L:(sub + 1) * L]) for sub in subs}

    ci_l = [d0 + h for _, h in units]
    cf_l = [2 * A_HEADS + d0 + h for _, h in units]
    i_col_l = [gc_ref[r0:r0 + L, ci:ci + 1] for (r0, _), ci in zip(units, ci_l)]
    i_row_l = [gr_ref[ci:ci + 1, r0:r0 + L] for (r0, _), ci in zip(units, ci_l)]
    f_col_l = [fcol[r0 // L][:, cf:cf + 1] for (r0, _), cf in zip(units, cf_l)]
    f_row_l = [frow[r0 // L][cf:cf + 1, :] for (r0, _), cf in zip(units, cf_l)]
    b_col_l = [jnp.sum(jnp.where(in_win, f, 0.0), axis=1, keepdims=True) for f in f_row_l]
    b_row_l = [jnp.sum(jnp.where(in_win_t, f, 0.0), axis=0, keepdims=True) for f in f_col_l]
    tot_l = [jnp.sum(f, axis=1, keepdims=True) for f in f_row_l]
    logw_l = [jnp.where(in_win, bc - br + ir, -jnp.inf) for bc, br, ir in zip(b_col_l, b_row_l, i_row_l)]
    mx_l = [jnp.max(lw, axis=1, keepdims=True) for lw in logw_l]
    p_l = [jnp.exp(lw - mx) for lw, mx in zip(logw_l, mx_l)]
    rep = lambda x: jnp.broadcast_to(x, (x.shape[0], A_DV))
    b_col_l = [rep(b) for b in b_col_l]
    mx_l = [rep(m) for m in mx_l]
    tot_l = [rep(t) for t in tot_l]
    log_g_l = [t - bc + rep(ic) for t, bc, ic in zip(tot_l, b_col_l, i_col_l)]
    mg_l = [jnp.max(lg, axis=0, keepdims=True) for lg in log_g_l]
    wg_l = [jnp.exp(lg - mg)[:, :A_DK] for lg, mg in zip(log_g_l, mg_l)]
    q_l = [q_ref[r0:r0 + L, h * A_DK:(h + 1) * A_DK] for r0, h in units]
    k_l = [k_ref[r0:r0 + L, h * A_DK:(h + 1) * A_DK] * (A_DK ** -0.5) for r0, h in units]
    v_l = [v_ref[r0:r0 + L, h * A_DV:(h + 1) * A_DV].astype(BF16) for r0, h in units]
    qk_l = [_mxu(q, k, _NT) for q, k in zip(q_l, k_l)]
    sp_l = [qk * p for qk, p in zip(qk_l, p_l)]
    ssum_l = [rep(jnp.sum(sp, axis=1, keepdims=True)) for sp in sp_l]
    sv_l = [_mxu(sp, v) for sp, v in zip(sp_l, v_l)]
    kw_l = [k * wg for k, wg in zip(k_l, wg_l)]
    ksum_l = [jnp.sum(kw, axis=0, keepdims=True) for kw in kw_l]
    kv_l = [_mxu(kw, v, _TN) for kw, v in zip(kw_l, v_l)]

    for i0 in range(0, len(units), A_HEADS):
        idx = range(i0, i0 + A_HEADS)
        heads = range(A_HEADS)
        m_prev = [m_sc[h:h + 1, :] for h in heads]
        c_prev = [c_sc[h] for h in heads]
        n_prev = [n_sc[h:h + 1, :] for h in heads]
        qc = [_mxu(q_l[i], c) for i, c in zip(idx, c_prev)]
        qn = [rep(jnp.sum(q_l[i] * n, axis=1, keepdims=True)) for i, n in zip(idx, n_prev)]
        inter = [b_col_l[i] + m for i, m in zip(idx, m_prev)]
        m_row = [jnp.maximum(it, mx_l[i]) for i, it in zip(idx, inter)]
        corr = [jnp.exp(mx_l[i] - mr) for i, mr in zip(idx, m_row)]
        w_inter = [jnp.exp(it - mr) for it, mr in zip(inter, m_row)]
        floor = [jnp.exp(-mr) for mr in m_row]
        num = [cr * sv_l[i] + wi * x for i, cr, wi, x in zip(idx, corr, w_inter, qc)]
        den = [cr * ssum_l[i] + wi * x for i, cr, wi, x in zip(idx, corr, w_inter, qn)]
        m_new = [jnp.maximum(tot_l[i] + m, mg_l[i]) for i, m in zip(idx, m_prev)]
        decay = [jnp.exp(tot_l[i] + m - mn) for i, m, mn in zip(idx, m_prev, m_new)]
        gain = [jnp.exp(mg_l[i] - mn) for i, mn in zip(idx, m_new)]
        for i, h in zip(idx, heads):
            r0 = units[i][0]
            h_ref[r0:r0 + L, h * A_DV:(h + 1) * A_DV] = num[h] / jnp.maximum(jnp.abs(den[h]), floor[h])
            c_sc[h] = decay[h] * c_prev[h] + gain[h] * kv_l[i]
            n_sc[h:h + 1, :] = decay[h][:, :A_DK] * n_prev[h] + gain[h][:, :A_DK] * ksum_l[i]
            m_sc[h:h + 1, :] = m_new[h]

    cf_ref[...] = c_sc[...]
    nf_ref[...] = n_sc[...]
    mf_ref[...] = m_sc[...]


def _scan_schedule(seq_lens, cb, reverse):
    blk, seq, first = [], [], []
    base = 0
    for s, n in enumerate(seq_lens):
        nb = n // cb
        order = range(nb - 1, -1, -1) if reverse else range(nb)
        for j, b in enumerate(order):
            blk.append(base + b)
            seq.append(s)
            first.append(1 if j == 0 else 0)
        base += nb
    as_i32 = lambda a: jnp.asarray(a, dtype=jnp.int32)
    return as_i32(blk), as_i32(seq), as_i32(first)


def _mlstm_scan(z, gates, gates_t, c0, n0, m0, *, seq_lens, reverse):
    cb = A_SCAN_ROWS
    nseq = len(seq_lens)
    rows = sum(seq_lens)
    blk, seq, first = _scan_schedule(seq_lens, cb, reverse)
    tile = lambda cols, col_block: pl.BlockSpec((cb, cols), lambda t, blk, seq, first: (blk[t], col_block))
    state = lambda *shape: pl.BlockSpec((None,) + shape, lambda t, blk, seq, first: (seq[t],) + (0,) * len(shape))
    grid_spec = pltpu.PrefetchScalarGridSpec(
        num_scalar_prefetch=3,
        grid=(blk.shape[0],),
        in_specs=[
            tile(A_QK, 0), tile(A_QK, 1), tile(A_V, 1), tile(A_GATE_COLS, 0),
            pl.BlockSpec((4 * A_HEADS, cb), lambda t, blk, seq, first: (0, blk[t])),
            state(A_HEADS, A_DK, A_DV), state(A_HEADS, A_DK), state(A_HEADS, A_DV),
        ],
        out_specs=[tile(A_V, 0), state(A_HEADS, A_DK, A_DV), state(A_HEADS, A_DK), state(A_HEADS, A_DV)],
        scratch_shapes=[
            pltpu.VMEM((A_HEADS, A_DK, A_DV), F32),
            pltpu.VMEM((A_HEADS, A_DK), F32),
            pltpu.VMEM((A_HEADS, A_DV), F32),
        ],
    )
    kern = functools.partial(_mlstm_scan_kernel, reverse=reverse, n_sub=cb // A_CHUNK)
    return pl.pallas_call(
        kern,
        grid_spec=grid_spec,
        out_shape=[
            jax.ShapeDtypeStruct((rows, A_V), F32),
            jax.ShapeDtypeStruct((nseq, A_HEADS, A_DK, A_DV), F32),
            jax.ShapeDtypeStruct((nseq, A_HEADS, A_DK), F32),
            jax.ShapeDtypeStruct((nseq, A_HEADS, A_DV), F32),
        ],
        compiler_params=_cparams("arbitrary"),
        name="mlstm_scan_bwd" if reverse else "mlstm_scan_fwd",
    )(blk, seq, first, z, z, z, gates, gates_t, c0, n0, m0)


def _rwkv_scan_kernel(blk_ref, seq_ref, first_ref, r_ref, k_ref, v_ref, lw_ref, a_ref, kk_ref, ka_ref, rk_ref,
                      s0_ref, y_ref, sf_ref, s_sc, *, reverse, n_sub):
    L = B_CHUNK
    W = B_PAIR_W
    R = 2 * L

    @pl.when(first_ref[pl.program_id(0)] == 1)
    def _():
        s_sc[...] = s0_ref[...]

    lane = lax.broadcasted_iota(jnp.int32, (L, W), 1)
    head0 = lane < B_HEAD_DIM
    ri = lax.broadcasted_iota(jnp.int32, (R, R), 0)
    ci = lax.broadcasted_iota(jnp.int32, (R, R), 1)
    same_head = (ri >= L) == (ci >= L)
    strict = same_head & ((ri < ci) if reverse else (ri > ci))
    incl = same_head & ((ri <= ci) if reverse else (ri >= ci))
    eye = (ri == ci).astype(F32)
    wi = lax.broadcasted_iota(jnp.int32, (W, W), 0)
    wj = lax.broadcasted_iota(jnp.int32, (W, W), 1)
    bd = (wi >= B_HEAD_DIM) == (wj >= B_HEAD_DIM)
    bd_ones = jnp.where(bd, 1.0, 0.0).astype(BF16)
    ti = lax.broadcasted_iota(jnp.int32, (L, L), 0)
    tj = lax.broadcasted_iota(jnp.int32, (L, L), 1)
    tri = jnp.where((tj >= ti) if reverse else (tj <= ti), 1.0, 0.0).astype(BF16)

    def stack_masked(x):
        return jnp.concatenate([jnp.where(head0, x, 0.0), jnp.where(head0, 0.0, x)], axis=0)

    def stack_dup(x):
        return jnp.concatenate([x, x], axis=0)

    subs = list(range(n_sub - 1, -1, -1) if reverse else range(n_sub))
    units = [(sub * L, slice(p * W, (p + 1) * W)) for sub in subs for p in range(B_PAIRS)]
    nu = len(units)

    def each(fn, *lists):
        return [fn(*args) for args in zip(*lists)]

    r_l = [r_ref[r0:r0 + L, sl].astype(F32) for r0, sl in units]
    k_l = [k_ref[r0:r0 + L, sl].astype(F32) for r0, sl in units]
    v_l = [v_ref[r0:r0 + L, sl].astype(F32) for r0, sl in units]
    lw_l = [-B_DECAY_SCALE * jax.nn.sigmoid(lw_ref[r0:r0 + L, sl]) for r0, sl in units]
    a_l = [jax.nn.sigmoid(a_ref[r0:r0 + L, sl]) for r0, sl in units]
    kkr_l = [k * kk_ref[:, sl] for k, (_, sl) in zip(k_l, units)]
    kd_l = [k * (1.0 + (a - 1.0) * ka_ref[:, sl]) for k, a, (_, sl) in zip(k_l, a_l, units)]
    ss_l = [_mxu(kkr * kkr, bd_ones, na=2) for kkr in kkr_l]
    bsum_l = [_mxu(r * kd * rk_ref[:, sl], bd_ones, na=2) for r, kd, (_, sl) in zip(r_l, kd_l, units)]
    c_inc_l = [_mxu(tri, lw, nb=2) for lw in lw_l]
    kk_l = each(lambda kkr, ss: kkr / jnp.maximum(jnp.sqrt(ss), 1e-12), kkr_l, ss_l)
    bv_l = each(lambda kk, a: kk * a, kk_l, a_l)
    bonus_l = each(lambda bs, v: bs * v, bsum_l, v_l)
    c_mid_l = [c[L // 2:L // 2 + 1, :] for c in c_inc_l]
    tot_l = [(c[0:1, :] if reverse else c[L - 1:L, :]) for c in c_inc_l]
    x_all_l = each(lambda kk, r, c, lw: jnp.concatenate(
        [stack_masked(-kk * jnp.exp(c - lw)), stack_masked(r * jnp.exp(c))], axis=0), kk_l, r_l, c_inc_l, lw_l)
    xa_l = each(lambda x, cm: (x * jnp.exp(-cm)).astype(BF16), x_all_l, c_mid_l)
    x_all_l = [x.astype(BF16) for x in x_all_l]

    def decayed(bv, kd, e):
        return jnp.concatenate([stack_dup(bv * e), stack_dup(kd * e)], axis=0).astype(BF16)

    y_all_l = each(lambda bv, kd, c, cm: decayed(bv, kd, jnp.exp(cm - c)), bv_l, kd_l, c_inc_l, c_mid_l)
    y_end_l = each(lambda bv, kd, c, t: decayed(bv, kd, jnp.exp(t - c)), bv_l, kd_l, c_inc_l, tot_l)
    v_s_l = [stack_masked(v).astype(BF16) for v in v_l]

    a_all_l = each(lambda x, y: _mxu(x, y, _NT), xa_l, y_all_l)
    a_ab_l = [jnp.where(strict, a[:R, :R], 0.0) for a in a_all_l]
    a_ak_l = [jnp.where(strict, a[:R, R:], 0.0).astype(BF16) for a in a_all_l]
    a_r_l = [jnp.concatenate([jnp.where(incl, a[R:, :R], 0.0), jnp.where(incl, a[R:, R:], 0.0)],
                             axis=1).astype(BF16) for a in a_all_l]
    akv_l = each(lambda a, v: _mxu(a, v, _NN), a_ak_l, v_s_l)

    t_inv_l = [eye + a for a in a_ab_l]
    pw_l = [_mxu(a, a, _NN) for a in a_ab_l]
    for _ in range(4):
        both_l = each(lambda pw, t: _mxu(pw, jnp.concatenate([pw, t], axis=1), _NN), pw_l, t_inv_l)
        pw_l = [b[:, :R] for b in both_l]
        t_inv_l = each(lambda t, b: t + b[:, R:], t_inv_l, both_l)
    t_inv_l = each(lambda t, pw: (t + _mxu(pw, t, _NN)).astype(BF16), t_inv_l, pw_l)

    for i0 in range(0, nu, B_PAIRS):
        idx = list(range(i0, i0 + B_PAIRS))
        s_prev_l = [s_sc[p] for p in range(B_PAIRS)]
        xs_l = [_mxu(x_all_l[i], s, _NT) for i, s in zip(idx, s_prev_l)]
        u_l = [_mxu(t_inv_l[i], xs[:R] + akv_l[i], _NN) for i, xs in zip(idx, xs_l)]
        uv_l = [jnp.concatenate([u.astype(BF16), v_s_l[i]], axis=0) for i, u in zip(idx, u_l)]
        y_s_l = [xs[R:] + _mxu(a_r_l[i], uv, _NN) for i, xs, uv in zip(idx, xs_l, uv_l)]
        for i, y_s in zip(idx, y_s_l):
            r0, sl = units[i]
            y_ref[r0:r0 + L, sl] = y_s[:L] + y_s[L:] + bonus_l[i]
        s_new_l = [s * jnp.exp(tot_l[i]) + _mxu(uv, y_end_l[i], _TN)
                   for i, s, uv in zip(idx, s_prev_l, uv_l)]
        for p, s_new in enumerate(s_new_l):
            s_sc[p] = jnp.where(bd, s_new, 0.0)

    sf_ref[...] = s_sc[...]


def _rwkv_scan(r, k, v, lw, a, k_k, k_a, r_k, s0, *, seq_lens, reverse):
    cb = B_SCAN_ROWS
    nseq = len(seq_lens)
    rows = sum(seq_lens)
    blk, seq, first = _scan_schedule(seq_lens, cb, reverse)
    tile = pl.BlockSpec((cb, D_MODEL), lambda t, blk, seq, first: (blk[t], 0))
    vec = pl.BlockSpec((1, D_MODEL), lambda t, blk, seq, first: (0, 0))
    st = pl.BlockSpec((None, B_PAIRS, B_PAIR_W, B_PAIR_W), lambda t, blk, seq, first: (seq[t], 0, 0, 0))
    grid_spec = pltpu.PrefetchScalarGridSpec(
        num_scalar_prefetch=3,
        grid=(blk.shape[0],),
        in_specs=[tile, tile, tile, tile, tile, vec, vec, vec, st],
        out_specs=[tile, st],
        scratch_shapes=[pltpu.VMEM((B_PAIRS, B_PAIR_W, B_PAIR_W), F32)],
    )
    kern = functools.partial(_rwkv_scan_kernel, reverse=reverse, n_sub=cb // B_CHUNK)
    return pl.pallas_call(
        kern,
        grid_spec=grid_spec,
        out_shape=[
            jax.ShapeDtypeStruct((rows, D_MODEL), F32),
            jax.ShapeDtypeStruct((nseq, B_PAIRS, B_PAIR_W, B_PAIR_W), F32),
        ],
        compiler_params=_cparams("arbitrary"),
        name="rwkv_scan_bwd" if reverse else "rwkv_scan_fwd",
    )(blk, seq, first, r, k, v, lw, a, k_k, k_a, r_k, s0)


def _pair_blockdiag(s):
    n = s.shape[0]
    s = s.reshape(n, B_PAIRS, 2, B_HEAD_DIM, B_HEAD_DIM)
    z = jnp.zeros_like(s[:, :, 0])
    top = jnp.concatenate([s[:, :, 0], z], axis=-1)
    bot = jnp.concatenate([z, s[:, :, 1]], axis=-1)
    return jnp.concatenate([top, bot], axis=-2)


def _pair_unblock(sp):
    n = sp.shape[0]
    h0 = sp[:, :, :B_HEAD_DIM, :B_HEAD_DIM]
    h1 = sp[:, :, B_HEAD_DIM:, B_HEAD_DIM:]
    return jnp.stack([h0, h1], axis=2).reshape(n, B_HEADS, B_HEAD_DIM, B_HEAD_DIM)


TM = 256
TM_WIDE = 512
TM_MOE = 1024
MOE_SUB = 512
MOE_CAP = 160


def _mod_spec(layer, tm):
    per_group = GROUP_ROWS // tm
    return pl.BlockSpec((None, None, N_ADA, D_MODEL), lambda i, *_: (layer, i // per_group, 0, 0))


def _row_spec(tm, cols, col_block=0):
    return pl.BlockSpec((tm, cols), lambda i, *_: (i, col_block))


def _full_spec(shape):
    nd = len(shape)
    return pl.BlockSpec(shape, lambda i, *_: (0,) * nd)


def _split_specs(tm, cols, n_first):
    t_first = n_first // tm
    return (pl.BlockSpec((tm, cols), lambda i, *_: (jnp.minimum(i, t_first - 1), 0)),
            pl.BlockSpec((tm, cols), lambda i, *_: (jnp.maximum(i - t_first, 0), 0)))


def _in_proj_kernel(xp_ref, xs_ref, mod_ref, g_ref, w_ref, b_ref, wg_ref, bg_ref, z_ref, gate_ref, *, ctx_tiles):
    x = jnp.where(pl.program_id(0) < ctx_tiles, xp_ref[...], xs_ref[...])
    hn = _norm_mod(x, g_ref[...], mod_ref[1:2, :], mod_ref[0:1, :])
    z_ref[...] = _dot(hn.astype(BF16), w_ref[...]) + b_ref[...]
    gate_ref[...] = _mxu(hn, wg_ref[...], na=2, nb=2) + bg_ref[...]


def _in_proj(xp, xs, mod, g1, w_main, b_main, w_gate, b_gate):
    rows = xp.shape[0] + xs.shape[0]
    cols = w_main.shape[1]
    tm = TM_WIDE
    return pl.pallas_call(
        functools.partial(_in_proj_kernel, ctx_tiles=xp.shape[0] // tm),
        grid=(rows // tm,),
        in_specs=[
            *_split_specs(tm, D_MODEL, xp.shape[0]), _mod_spec(0, tm), _full_spec((1, D_MODEL)),
            _full_spec(w_main.shape), _full_spec(b_main.shape), _full_spec(w_gate.shape), _full_spec(b_gate.shape),
        ],
        out_specs=[_row_spec(tm, cols), _row_spec(tm, A_GATE_COLS)],
        out_shape=[jax.ShapeDtypeStruct((rows, cols), F32), jax.ShapeDtypeStruct((rows, A_GATE_COLS), F32)],
        compiler_params=_cparams("parallel"),
        name="mlstm_in_proj",
    )(xp, xs, mod, g1, w_main, b_main, w_gate, b_gate)


def _mlstm_out_ff_kernel(hf_ref, hb_ref, o_ref, xp_ref, xs_ref, mod_ref, modn_ref, gh_ref, wo_ref, g2_ref,
                         wgu_ref, wdn_ref, gn_ref, x_out_ref, hn_out_ref, *, ctx_tiles):
    x = jnp.where(pl.program_id(0) < ctx_tiles, xp_ref[...], xs_ref[...])
    hs = hf_ref[...] + hb_ref[...]
    parts = []
    for h in range(A_HEADS):
        seg = hs[:, h * A_DV:(h + 1) * A_DV]
        parts.append(seg * lax.rsqrt(jnp.mean(seg * seg, axis=-1, keepdims=True) + RMS_EPS))
    y = (jnp.concatenate(parts, axis=1) * gh_ref[...]) * jax.nn.sigmoid(o_ref[...])
    x1 = x + mod_ref[2:3, :] * _dot(y.astype(BF16), wo_ref[...])
    hn2 = _norm_mod(x1, g2_ref[...], mod_ref[4:5, :], mod_ref[3:4, :])
    gu = _dot(hn2.astype(BF16), wgu_ref[...])
    act = jax.nn.silu(gu[:, :D_FF]) * gu[:, D_FF:]
    x2 = x1 + mod_ref[5:6, :] * _dot(act.astype(BF16), wdn_ref[...])
    x_out_ref[...] = x2
    hn_out_ref[...] = _norm_mod(x2, gn_ref[...], modn_ref[1:2, :], modn_ref[0:1, :])


def _mlstm_out_ff(hf, hb, z, xp, xs, mod, g_head, w_out, g2, w_gu, w_down, g1_next):
    rows = xp.shape[0] + xs.shape[0]
    return pl.pallas_call(
        functools.partial(_mlstm_out_ff_kernel, ctx_tiles=xp.shape[0] // TM),
        grid=(rows // TM,),
        in_specs=[
            _row_spec(TM, A_V), _row_spec(TM, A_V), _row_spec(TM, A_V, 2), *_split_specs(TM, D_MODEL, xp.shape[0]),
            _mod_spec(0, TM), _mod_spec(1, TM), _full_spec((1, A_V)), _full_spec(w_out.shape),
            _full_spec((1, D_MODEL)), _full_spec(w_gu.shape), _full_spec(w_down.shape), _full_spec((1, D_MODEL)),
        ],
        out_specs=[_row_spec(TM, D_MODEL), _row_spec(TM, D_MODEL)],
        out_shape=[jax.ShapeDtypeStruct((rows, D_MODEL), F32)] * 2,
        compiler_params=_cparams("parallel"),
        name="mlstm_out_ff",
    )(hf, hb, z, xp, xs, mod, mod, g_head, w_out, g2, w_gu, w_down, g1_next)


def _token_shift(cur, up, dn, is_ctx, first, last):
    tm = cur.shape[0]
    q = D_MODEL // 4
    row = lax.broadcasted_iota(jnp.int32, (tm, q), 0)
    col = row % GRID_W
    prev = pltpu.roll(cur, 1, 0)
    nxt = pltpu.roll(cur, tm - 1, 0)
    q0 = jnp.where(jnp.where(is_ctx, row, col) == 0, 0.0, prev[:, :q])
    q1 = jnp.where(is_ctx, jnp.where(row == 0, 0.0, prev[:, q:2 * q]),
                   jnp.where(col == GRID_W - 1, 0.0, nxt[:, q:2 * q]))
    upper = jnp.concatenate([up[:, 2 * q:3 * q], cur[:tm - GRID_W, 2 * q:3 * q]], axis=0)
    upper = jnp.where(first & (row < GRID_W), 0.0, upper)
    q2 = jnp.where(is_ctx, jnp.where(row == tm - 1, 0.0, nxt[:, 2 * q:3 * q]), upper)
    lower = jnp.concatenate([cur[GRID_W:, 3 * q:], dn[:, 3 * q:]], axis=0)
    lower = jnp.where(last & (row >= tm - GRID_W), 0.0, lower)
    q3 = jnp.where(is_ctx, jnp.where(row == tm - 1, 0.0, nxt[:, 3 * q:]), lower)
    return jnp.concatenate([q0, q1, q2, q3], axis=1)


def _rwkv_in_kernel(hn_ref, up_ref, dn_ref, mu_ref, wrkv_ref, w1_ref, w2_ref, w0_ref, a1_ref, a2_ref, a0_ref,
                    g1_ref, g2_ref, r_ref, k_ref, v_ref, lw0_ref, lw1_ref, ia0_ref, ia1_ref, g_ref,
                    *, ctx_tiles, tiles_per_seq):
    i = pl.program_id(0)
    j = (i - ctx_tiles) % tiles_per_seq
    h = hn_ref[...]
    xx = _token_shift(h, up_ref[...], dn_ref[...], i < ctx_tiles, j == 0, j == tiles_per_seq - 1) - h

    def mix(c):
        return (h + xx * mu_ref[c:c + 1, :]).astype(BF16)

    r_ref[...] = _dot(mix(0), wrkv_ref[0]).astype(BF16)
    k_ref[...] = _dot(mix(1), wrkv_ref[1]).astype(BF16)
    v_ref[...] = _dot(mix(2), wrkv_ref[2]).astype(BF16)
    lane = lax.broadcasted_iota(jnp.int32, (h.shape[0], 2 * 64), 1)
    dir0 = lane < 64
    t = jnp.tanh(_dot(mix(3), w1_ref[...]))
    lw0_ref[...] = w0_ref[0:1, :] + _dot(jnp.where(dir0, t, 0.0).astype(BF16), w2_ref[...])
    lw1_ref[...] = w0_ref[1:2, :] + _dot(jnp.where(dir0, 0.0, t).astype(BF16), w2_ref[...])
    u = _dot(mix(4), a1_ref[...])
    ia0_ref[...] = a0_ref[0:1, :] + _dot(jnp.where(dir0, u, 0.0).astype(BF16), a2_ref[...])
    ia1_ref[...] = a0_ref[1:2, :] + _dot(jnp.where(dir0, 0.0, u).astype(BF16), a2_ref[...])
    g_ref[...] = _dot(jax.nn.sigmoid(_dot(mix(5), g1_ref[...])).astype(BF16), g2_ref[...]).astype(BF16)


def _rwkv_in(hn, mu, w_rkv, w1, w2, w0, a1, a2, a0, g1, g2, *, n_ctx, ctx_len, lat_len):
    rows = hn.shape[0]
    assert ctx_len == TM and lat_len % TM == 0 and TM % GRID_W == 0
    per_tile = TM // GRID_W
    n_halo = rows // GRID_W
    ins = [mu, w_rkv, w1, w2, w0, a1, a2, a0, g1, g2]
    kern = functools.partial(_rwkv_in_kernel, ctx_tiles=n_ctx // TM, tiles_per_seq=lat_len // TM)
    return pl.pallas_call(
        kern,
        grid=(rows // TM,),
        in_specs=[
            _row_spec(TM, D_MODEL),
            pl.BlockSpec((GRID_W, D_MODEL), lambda i: (jnp.maximum(i * per_tile - 1, 0), 0)),
            pl.BlockSpec((GRID_W, D_MODEL), lambda i: (jnp.minimum((i + 1) * per_tile, n_halo - 1), 0)),
        ] + [_full_spec(a.shape) for a in ins],
        out_specs=[_row_spec(TM, D_MODEL)] * 8,
        out_shape=[jax.ShapeDtypeStruct((rows, D_MODEL), dt) for dt in (BF16, BF16, BF16, F32, F32, F32, F32, BF16)],
        compiler_params=_cparams("parallel"),
        name="rwkv_in_proj",
    )(hn, hn, hn, *ins)


def _rwkv_out_kernel(yf_ref, yb_ref, g_ref, x_ref, mod_ref, lng_ref, lnb_ref, wo_ref, x_out_ref):
    y = yf_ref[...] + yb_ref[...]
    wi = lax.broadcasted_iota(jnp.int32, (B_PAIR_W, B_PAIR_W), 0)
    wj = lax.broadcasted_iota(jnp.int32, (B_PAIR_W, B_PAIR_W), 1)
    bd_mean = jnp.where((wi >= B_HEAD_DIM) == (wj >= B_HEAD_DIM), 1.0 / B_HEAD_DIM, 0.0)
    parts = []
    for p in range(B_PAIRS):
        seg = y[:, p * B_PAIR_W:(p + 1) * B_PAIR_W]
        cen = seg - _mxu(seg, bd_mean, na=3)
        var = _mxu(cen * cen, bd_mean, na=3)
        parts.append(cen * lax.rsqrt(var + B_LN_EPS))
    yn = jnp.concatenate(parts, axis=1) * lng_ref[...] + lnb_ref[...]
    out = _dot((yn * g_ref[...]).astype(BF16), wo_ref[...])
    x_out_ref[...] = x_ref[...] + mod_ref[2:3, :] * out


def _rwkv_out(yf, yb, g, x, mod, ln_g, ln_b, w_out):
    rows = x.shape[0]
    tm = TM_WIDE
    return pl.pallas_call(
        _rwkv_out_kernel,
        grid=(rows // tm,),
        in_specs=[_row_spec(tm, D_MODEL)] * 4 + [
            _mod_spec(1, tm), _full_spec((1, D_MODEL)), _full_spec((1, D_MODEL)), _full_spec(w_out.shape)],
        out_specs=_row_spec(tm, D_MODEL),
        out_shape=jax.ShapeDtypeStruct((rows, D_MODEL), F32),
        compiler_params=_cparams("parallel"),
        name="rwkv_out_proj",
    )(yf, yb, g, x, mod, ln_g, ln_b, w_out)


def _moe_kernel(x_ref, mod_ref, g2_ref, wr_ref, br_ref, upper_ref, wgu_ref, wdn_ref, gf_ref, out_p_ref, out_s_ref,
                hn_sc, sel_sc, cw_sc, rank_sc, acc_sc, *, ctx_tiles):
    e = pl.program_id(1)
    tm = x_ref.shape[0]
    n_sub = tm // MOE_SUB

    @pl.when(e == 0)
    def _():
        hn = _norm_mod(x_ref[...], g2_ref[...], mod_ref[4:5, :], mod_ref[3:4, :])
        hn_sc[...] = hn.astype(BF16)
        logits = _mxu(hn, wr_ref[...], na=3, nb=3) + br_ref[...]
        lane = lax.broadcasted_iota(jnp.int32, logits.shape, 1)
        m1 = jnp.max(logits, axis=1, keepdims=True)
        i1 = jnp.min(jnp.where(logits == m1, lane, ROUTER_COLS), axis=1, keepdims=True)
        rest = jnp.where(lane == i1, -jnp.inf, logits)
        m2 = jnp.max(rest, axis=1, keepdims=True)
        i2 = jnp.min(jnp.where(rest == m2, lane, ROUTER_COLS), axis=1, keepdims=True)
        e2 = jnp.exp(m2 - m1)
        den = 1.0 + e2
        comb = jnp.where(lane == i1, 1.0 / den, 0.0) + jnp.where(lane == i2, e2 / den, 0.0)
        sel = jnp.where((lane == i1) | (lane == i2), 1.0, 0.0)
        for s in range(n_sub):
            rows = slice(s * MOE_SUB, (s + 1) * MOE_SUB)
            sel_t = sel[rows, :].T[:N_EXPERTS, :]
            sel_sc[s] = sel_t
            cw_sc[s] = comb[rows, :].T[:N_EXPERTS, :]
            rank_sc[s] = _mxu(sel_t, upper_ref[...]).astype(jnp.int32)
        acc_sc[...] = jnp.zeros_like(acc_sc)

    for s in range(n_sub):
        rows = slice(s * MOE_SUB, (s + 1) * MOE_SUB)
        sel_row = sel_sc[s, pl.ds(e, 1), :]
        cw_row = cw_sc[s, pl.ds(e, 1), :]
        rank_row = rank_sc[s, pl.ds(e, 1), :]
        count = jnp.sum(sel_row).astype(jnp.int32)

        def evaluate(first_rank, cap):
            slot = lax.broadcasted_iota(jnp.int32, (cap, MOE_SUB), 0) + first_rank
            pick = jnp.where((rank_row == slot) & (sel_row > 0.5), 1.0, 0.0)
            pick_bf = pick.astype(BF16)
            xg = _mxu(pick_bf, hn_sc[rows, :])
            gu = _mxu(xg, wgu_ref[...])
            act = jax.nn.silu(gu[:, :D_FF_EXPERT]) * gu[:, D_FF_EXPERT:]
            ye = _mxu(act, wdn_ref[...])
            wj = jnp.sum(pick * cw_row, axis=1, keepdims=True)
            acc_sc[rows, :] += _mxu(pick_bf, wj * ye, _TN)

        def one_round(c, carry):
            evaluate(c * MOE_CAP, MOE_CAP)
            return carry

        lax.fori_loop(0, (count + MOE_CAP - 1) // MOE_CAP, one_round, 0)

    def final_norm():
        x4 = x_ref[...] + mod_ref[5:6, :] * acc_sc[...]
        return (x4 * lax.rsqrt(jnp.mean(x4 * x4, axis=-1, keepdims=True) + RMS_EPS)) * gf_ref[...]

    is_ctx = pl.program_id(0) < ctx_tiles

    @pl.when((e == N_EXPERTS - 1) & is_ctx)
    def _():
        out_p_ref[...] = final_norm()

    @pl.when((e == N_EXPERTS - 1) & jnp.logical_not(is_ctx))
    def _():
        out_s_ref[...] = final_norm()


def _moe(x, mod, g2, w_router, b_router, w_gu, w_down, g_final, *, n_ctx):
    rows = x.shape[0]
    tm = TM_MOE
    n_sub = tm // MOE_SUB
    upper = jnp.triu(jnp.ones((MOE_SUB, MOE_SUB), BF16), k=1)
    return pl.pallas_call(
        functools.partial(_moe_kernel, ctx_tiles=n_ctx // tm),
        grid=(rows // tm, N_EXPERTS),
        in_specs=[
            _row_spec(tm, D_MODEL), _mod_spec(1, tm), _full_spec((1, D_MODEL)),
            _full_spec(w_router.shape), _full_spec(b_router.shape), _full_spec(upper.shape),
            pl.BlockSpec((None, D_MODEL, 2 * D_FF_EXPERT), lambda i, e: (e, 0, 0)),
            pl.BlockSpec((None, D_FF_EXPERT, D_MODEL), lambda i, e: (e, 0, 0)),
            _full_spec((1, D_MODEL)),
        ],
        out_specs=list(_split_specs(tm, D_MODEL, n_ctx)),
        out_shape=[jax.ShapeDtypeStruct((n_ctx, D_MODEL), F32), jax.ShapeDtypeStruct((rows - n_ctx, D_MODEL), F32)],
        scratch_shapes=[
            pltpu.VMEM((tm, D_MODEL), BF16),
            pltpu.VMEM((n_sub, N_EXPERTS, MOE_SUB), F32),
            pltpu.VMEM((n_sub, N_EXPERTS, MOE_SUB), F32),
            pltpu.VMEM((n_sub, N_EXPERTS, MOE_SUB), jnp.int32),
            pltpu.VMEM((tm, D_MODEL), F32),
        ],
        compiler_params=_cparams("arbitrary", "arbitrary"),
        name="moe_final",
    )(x, mod, g2, w_router, b_router, upper, w_gu, w_down, g_final)


def _pad_cols(a, cols, value=0.0):
    return jnp.pad(a, ((0, 0), (0, cols - a.shape[1])), constant_values=value)


def kernel(x_prompt, x_sample, state_mlstm_C, state_mlstm_n, state_mlstm_m, state_rwkv_S, c, c_ctx, w_ada, b_ada, g_norm1, g_norm2, w_m_in, b_m_in, g_m_head, w_m_out, mu_rwkv, w_rkv, w_decay0, w_decay1, w_decay2, w_iclr0, w_iclr1, w_iclr2, w_gate1, w_gate2, k_k, k_a, r_k, g_ln_x, b_ln_x, w_rwkv_out, w_ff_gu, w_ff_down, w_router, b_router, w_exp_gu, w_exp_down, g_final):
    nb, seq, d = x_prompt.shape
    db, dseq, _ = x_sample.shape
    n_ctx = nb * seq
    n_lat = db * dseq
    assert d == D_MODEL and n_ctx == GROUP_ROWS and dseq == GROUP_ROWS

    xp = x_prompt.reshape(n_ctx, d)
    xs = x_sample.reshape(n_lat, d)
    cond = jnp.concatenate([c_ctx[None, :], c], axis=0)
    n_groups = cond.shape[0]
    cond = jnp.pad(cond, ((0, (-n_groups) % 8), (0, 0)))
    mod = _ada(cond, w_ada, b_ada).reshape(w_ada.shape[0], cond.shape[0], N_ADA, d)
    row = lambda a: a.reshape(1, -1)

    n_main = 2 * A_QK + 2 * A_V
    w_in = w_m_in[0]
    z, gates = _in_proj(xp, xs, mod, row(g_norm1[0]), w_in[:, :n_main].astype(BF16), row(b_m_in[0][:n_main]),
                        _pad_cols(w_in[:, n_main:], A_GATE_COLS), _pad_cols(row(b_m_in[0][n_main:]), A_GATE_COLS))
    gates_t = gates[:, :4 * A_HEADS].T
    seq_lens = (seq,) * nb + (dseq,) * db
    with_zero_ctx = lambda s: jnp.concatenate([jnp.zeros((nb,) + s.shape[1:], F32), s.astype(F32)], axis=0)
    h_dirs, ctx_state = [], []
    for dirn, rev in ((0, False), (1, True)):
        h_d, cf, nf, mf = _mlstm_scan(
            z, gates, gates_t, with_zero_ctx(state_mlstm_C[:, 0, dirn]), with_zero_ctx(state_mlstm_n[:, 0, dirn]),
            with_zero_ctx(jnp.broadcast_to(state_mlstm_m[:, 0, dirn][..., None], (db, A_HEADS, A_DV))),
            seq_lens=seq_lens, reverse=rev)
        h_dirs.append(h_d)
        ctx_state.append((cf[:nb], nf[:nb], mf[:nb, :, 0]))
    new_c = jnp.stack([s[0] for s in ctx_state], axis=1)[:, None]
    new_n = jnp.stack([s[1] for s in ctx_state], axis=1)[:, None]
    new_m = jnp.stack([s[2] for s in ctx_state], axis=1)[:, None]
    x, hn = _mlstm_out_ff(h_dirs[0], h_dirs[1], z, xp, xs, mod, row(g_m_head[0]), w_m_out[0].astype(BF16),
                          row(g_norm2[0]), w_ff_gu[0].astype(BF16), w_ff_down[0].astype(BF16), row(g_norm1[1]))

    cat_cols = lambda w: jnp.concatenate([w[0], w[1]], axis=1).astype(BF16)
    cat_rows = lambda w: jnp.concatenate([w[0], w[1]], axis=0).astype(BF16)
    r, k, v, lw0, lw1, ia0, ia1, g = _rwkv_in(
        hn, mu_rwkv[0], w_rkv[0].astype(BF16), cat_cols(w_decay1[0]), cat_rows(w_decay2[0]), w_decay0[0],
        cat_cols(w_iclr1[0]), cat_rows(w_iclr2[0]), w_iclr0[0], w_gate1[0].astype(BF16), w_gate2[0].astype(BF16),
        n_ctx=n_ctx, ctx_len=seq, lat_len=dseq)
    y_dirs, ctx_s = [], []
    for dirn, rev, lw, ia in ((0, False, lw0, ia0), (1, True, lw1, ia1)):
        y_d, sf = _rwkv_scan(r, k, v, lw, ia, row(k_k[0]), row(k_a[0]), row(r_k[0]),
                             with_zero_ctx(_pair_blockdiag(state_rwkv_S[:, 0, dirn])), seq_lens=seq_lens, reverse=rev)
        y_dirs.append(y_d)
        ctx_s.append(_pair_unblock(sf[:nb]))
    new_s = jnp.stack(ctx_s, axis=1)[:, None]
    x = _rwkv_out(y_dirs[0], y_dirs[1], g, x, mod, row(g_ln_x[0]), row(b_ln_x[0]), w_rwkv_out[0].astype(BF16))
    y_p, y_s = _moe(x, mod, row(g_norm2[1]), _pad_cols(w_router[0], ROUTER_COLS),
                    _pad_cols(row(b_router[0]), ROUTER_COLS, NEG_BIG), w_exp_gu[0].astype(BF16),
                    w_exp_down[0].astype(BF16), row(g_final), n_ctx=n_ctx)
    return (y_p.reshape(nb, seq, d), y_s.reshape(db, dseq, d), new_c, new_n, new_m, new_s)
```

```python
import functools

import jax
import jax.numpy as jnp
from jax import lax
from jax.experimental import pallas as pl
from jax.experimental.pallas import tpu as pltpu

F32 = jnp.float32
BF16 = jnp.bfloat16
HIGHEST = lax.Precision.HIGHEST

D_MODEL = 1024
GROUP_ROWS = 4096
N_ADA = 6
RMS_EPS = 1e-6
GRID_W = 64
A_HEADS = 8
A_DK = 64
A_DV = 128
A_QK = A_HEADS * A_DK
A_V = A_HEADS * A_DV
A_CHUNK = 128
A_GATE_COLS = 128
A_SCAN_ROWS = 256
B_HEADS = 16
B_HEAD_DIM = 64
B_PAIRS = B_HEADS // 2
B_PAIR_W = 2 * B_HEAD_DIM
B_CHUNK = 64
B_DECAY_SCALE = 0.606531
B_LN_EPS = 64e-5
B_SCAN_ROWS = 128
D_FF = 2816
N_EXPERTS = 8
D_FF_EXPERT = D_FF // 2
ROUTER_COLS = 128
NEG_BIG = -1e30

VMEM_LIMIT = 56 * 1024 * 1024


def _cparams(*sem):
    return pltpu.CompilerParams(dimension_semantics=sem, vmem_limit_bytes=VMEM_LIMIT)


def _dot(a, b, precision=None):
    return jnp.dot(a, b, precision=precision, preferred_element_type=F32)


def _dot_nt(a, b, precision=None):
    return lax.dot_general(a, b, (((1,), (1,)), ((), ())), precision=precision, preferred_element_type=F32)


def _dot_tn(a, b, precision=None):
    return lax.dot_general(a, b, (((0,), (0,)), ((), ())), precision=precision, preferred_element_type=F32)


_NN = (((1,), (0,)), ((), ()))
_NT = (((1,), (1,)), ((), ()))
_TN = (((0,), (0,)), ((), ()))


def _bf16_terms(a, n):
    terms = []
    for i in range(n):
        t = a.astype(BF16)
        terms.append(t)
        if i + 1 < n:
            a = a - t.astype(F32)
    return terms


def _mxu(a, b, dims=_NN, na=1, nb=1):
    at = _bf16_terms(a, na)
    bt = _bf16_terms(b, nb)
    out = None
    for i, x in enumerate(at):
        for j, y in enumerate(bt):
            if i + j < max(na, nb):
                t = lax.dot_general(x, y, dims, preferred_element_type=F32)
                out = t if out is None else out + t
    return out


def _norm_mod(x, g, sc, sh):
    y = x * lax.rsqrt(jnp.mean(x * x, axis=-1, keepdims=True) + RMS_EPS)
    return (y * g) * (1.0 + sc) + sh


def _log_sigmoid(x):
    return jnp.minimum(x, 0.0) - jnp.log1p(jnp.exp(-jnp.abs(x)))


def _ada_kernel(cond_ref, w_ref, b_ref, out_ref):
    c = cond_ref[...]
    out_ref[...] = _dot(c * jax.nn.sigmoid(c), w_ref[...], HIGHEST) + b_ref[...]


def _ada(cond_pad, w_ada, b_ada):
    depth, d, cols = w_ada.shape
    rows = cond_pad.shape[0]
    tn = 1536
    return pl.pallas_call(
        _ada_kernel,
        grid=(depth, cols // tn),
        in_specs=[
            pl.BlockSpec((rows, d), lambda l, j: (0, 0)),
            pl.BlockSpec((None, d, tn), lambda l, j: (l, 0, j)),
            pl.BlockSpec((None, 1, tn), lambda l, j: (l, 0, j)),
        ],
        out_specs=pl.BlockSpec((None, rows, tn), lambda l, j: (l, 0, j)),
        out_shape=jax.ShapeDtypeStruct((depth, rows, cols), F32),
        compiler_params=_cparams("parallel", "parallel"),
        name="ada",
    )(cond_pad, w_ada, b_ada.reshape(depth, 1, cols))


def _mlstm_scan_kernel(blk_ref, seq_ref, first_ref, q_ref, k_ref, v_ref, gc_ref, gr_ref, c0_ref, n0_ref, m0_ref,
                       h_ref, cf_ref, nf_ref, mf_ref, c_sc, n_sc, m_sc, *, reverse, n_sub):
    L = A_CHUNK

    @pl.when(first_ref[pl.program_id(0)] == 1)
    def _():
        c_sc[...] = c0_ref[...]
        n_sc[...] = n0_ref[...]
        m_sc[...] = m0_ref[...]

    row = lax.broadcasted_iota(jnp.int32, (L, L), 0)
    col = lax.broadcasted_iota(jnp.int32, (L, L), 1)
    in_win = (col >= row) if reverse else (col <= row)
    in_win_t = (row >= col) if reverse else (row <= col)
    d0 = A_HEADS if reverse else 0

    subs = list(range(n_sub - 1, -1, -1) if reverse else range(n_sub))
    units = [(sub * L, h) for sub in subs for h in range(A_HEADS)]
    fcol = {sub: _log_sigmoid(gc_ref[sub * L:(sub + 1) * L, :]) for sub in subs}
    frow = {sub: _log_sigmoid(gr_ref[:, sub * L:(sub + 1) * L]) for sub in subs}

    ci_l = [d0 + h for _, h in units]
    cf_l = [2 * A_HEADS + d0 + h for _, h in units]
    i_col_l = [gc_ref[r0:r0 + L, ci:ci + 1] for (r0, _), ci in zip(units, ci_l)]
    i_row_l = [gr_ref[ci:ci + 1, r0:r0 + L] for (r0, _), ci in zip(units, ci_l)]
    f_col_l = [fcol[r0 // L][:, cf:cf + 1] for (r0, _), cf in zip(units, cf_l)]
    f_row_l = [frow[r0 // L][cf:cf + 1, :] for (r0, _), cf in zip(units, cf_l)]
    b_col_l = [jnp.sum(jnp.where(in_win, f, 0.0), axis=1, keepdims=True) for f in f_row_l]
    b_row_l = [jnp.sum(jnp.where(in_win_t, f, 0.0), axis=0, keepdims=True) for f in f_col_l]
    tot_l = [jnp.sum(f, axis=1, keepdims=True) for f in f_row_l]
    logw_l = [jnp.where(in_win, bc - br + ir, -jnp.inf) for bc, br, ir in zip(b_col_l, b_row_l, i_row_l)]
    mx_l = [jnp.max(lw, axis=1, keepdims=True) for lw in logw_l]
    p_l = [jnp.exp(lw - mx) for lw, mx in zip(logw_l, mx_l)]
    rep = lambda x: jnp.broadcast_to(x, (x.shape[0], A_DV))
    b_col_l = [rep(b) for b in b_col_l]
    mx_l = [rep(m) for m in mx_l]
    tot_l = [rep(t) for t in tot_l]
    log_g_l = [t - bc + rep(ic) for t, bc, ic in zip(tot_l, b_col_l, i_col_l)]
    mg_l = [jnp.max(lg, axis=0, keepdims=True) for lg in log_g_l]
    wg_l = [jnp.exp(lg - mg)[:, :A_DK] for lg, mg in zip(log_g_l, mg_l)]
    q_l = [q_ref[r0:r0 + L, h * A_DK:(h + 1) * A_DK] for r0, h in units]
    k_l = [k_ref[r0:r0 + L, h * A_DK:(h + 1) * A_DK] * (A_DK ** -0.5) for r0, h in units]
    v_l = [v_ref[r0:r0 + L, h * A_DV:(h + 1) * A_DV].astype(BF16) for r0, h in units]
    qk_l = [_mxu(q, k, _NT) for q, k in zip(q_l, k_l)]
    sp_l = [qk * p for qk, p in zip(qk_l, p_l)]
    ssum_l = [rep(jnp.sum(sp, axis=1, keepdims=True)) for sp in sp_l]
    sv_l = [_mxu(sp, v) for sp, v in zip(sp_l, v_l)]
    kw_l = [k * wg for k, wg in zip(k_l, wg_l)]
    ksum_l = [jnp.sum(kw, axis=0, keepdims=True) for kw in kw_l]
    kv_l = [_mxu(kw, v, _TN) for kw, v in zip(kw_l, v_l)]

    nu = len(units)
    heads = range(A_HEADS)
    m_cur = [m_sc[h:h + 1, :] for h in heads]
    c_cur = [c_sc[h] for h in heads]
    n_cur = [n_sc[h:h + 1, :] for h in heads]
    m_prev, c_prev, n_prev = [None] * nu, [None] * nu, [None] * nu
    for i0 in range(0, nu, A_HEADS):
        for h in heads:
            i = i0 + h
            m_prev[i], c_prev[i], n_prev[i] = m_cur[h], c_cur[h], n_cur[h]
            m_new = jnp.maximum(tot_l[i] + m_cur[h], mg_l[i])
            decay = jnp.exp(tot_l[i] + m_cur[h] - m_new)
            gain = jnp.exp(mg_l[i] - m_new)
            c_cur[h] = decay * c_cur[h] + gain * kv_l[i]
            n_cur[h] = decay[:, :A_DK] * n_cur[h] + gain[:, :A_DK] * ksum_l[i]
            m_cur[h] = m_new
    for h in heads:
        c_sc[h] = c_cur[h]
        n_sc[h:h + 1, :] = n_cur[h]
        m_sc[h:h + 1, :] = m_cur[h]

    qc = [_mxu(q, c) for q, c in zip(q_l, c_prev)]
    qn = [rep(jnp.sum(q * n, axis=1, keepdims=True)) for q, n in zip(q_l, n_prev)]
    inter = [b + m for b, m in zip(b_col_l, m_prev)]
    m_row = [jnp.maximum(it, mx) for it, mx in zip(inter, mx_l)]
    corr = [jnp.exp(mx - mr) for mx, mr in zip(mx_l, m_row)]
    w_inter = [jnp.exp(it - mr) for it, mr in zip(inter, m_row)]
    floor = [jnp.exp(-mr) for mr in m_row]
    num = [cr * sv + wi * x for cr, sv, wi, x in zip(corr, sv_l, w_inter, qc)]
    den = [cr * ss + wi * x for cr, ss, wi, x in zip(corr, ssum_l, w_inter, qn)]
    for (r0, h), nm, dn, fl in zip(units, num, den, floor):
        h_ref[r0:r0 + L, h * A_DV:(h + 1) * A_DV] = nm / jnp.maximum(jnp.abs(dn), fl)

    cf_ref[...] = c_sc[...]
    nf_ref[...] = n_sc[...]
    mf_ref[...] = m_sc[...]


def _scan_schedule(seq_lens, cb, reverse):
    blk, seq, first = [], [], []
    base = 0
    for s, n in enumerate(seq_lens):
        nb = n // cb
        order = range(nb - 1, -1, -1) if reverse else range(nb)
        for j, b in enumerate(order):
            blk.append(base + b)
            seq.append(s)
            first.append(1 if j == 0 else 0)
        base += nb
    as_i32 = lambda a: jnp.asarray(a, dtype=jnp.int32)
    return as_i32(blk), as_i32(seq), as_i32(first)


def _mlstm_scan(z, gates, gates_t, c0, n0, m0, *, seq_lens, reverse):
    cb = A_SCAN_ROWS
    nseq = len(seq_lens)
    rows = sum(seq_lens)
    blk, seq, first = _scan_schedule(seq_lens, cb, reverse)
    tile = lambda cols, col_block: pl.BlockSpec((cb, cols), lambda t, blk, seq, first: (blk[t], col_block))
    state = lambda *shape: pl.BlockSpec((None,) + shape, lambda t, blk, seq, first: (seq[t],) + (0,) * len(shape))
    grid_spec = pltpu.PrefetchScalarGridSpec(
        num_scalar_prefetch=3,
        grid=(blk.shape[0],),
        in_specs=[
            tile(A_QK, 0), tile(A_QK, 1), tile(A_V, 1), tile(A_GATE_COLS, 0),
            pl.BlockSpec((4 * A_HEADS, cb), lambda t, blk, seq, first: (0, blk[t])),
            state(A_HEADS, A_DK, A_DV), state(A_HEADS, A_DK), state(A_HEADS, A_DV),
        ],
        out_specs=[tile(A_V, 0), state(A_HEADS, A_DK, A_DV), state(A_HEADS, A_DK), state(A_HEADS, A_DV)],
        scratch_shapes=[
            pltpu.VMEM((A_HEADS, A_DK, A_DV), F32),
            pltpu.VMEM((A_HEADS, A_DK), F32),
            pltpu.VMEM((A_HEADS, A_DV), F32),
        ],
    )
    kern = functools.partial(_mlstm_scan_kernel, reverse=reverse, n_sub=cb // A_CHUNK)
    return pl.pallas_call(
        kern,
        grid_spec=grid_spec,
        out_shape=[
            jax.ShapeDtypeStruct((rows, A_V), F32),
            jax.ShapeDtypeStruct((nseq, A_HEADS, A_DK, A_DV), F32),
            jax.ShapeDtypeStruct((nseq, A_HEADS, A_DK), F32),
            jax.ShapeDtypeStruct((nseq, A_HEADS, A_DV), F32),
        ],
        compiler_params=_cparams("arbitrary"),
        name="mlstm_scan_bwd" if reverse else "mlstm_scan_fwd",
    )(blk, seq, first, z, z, z, gates, gates_t, c0, n0, m0)


def _rwkv_scan_kernel(blk_ref, seq_ref, first_ref, r_ref, k_ref, v_ref, lw_ref, a_ref, kk_ref, ka_ref, rk_ref,
                      s0_ref, y_ref, sf_ref, s_sc, *, reverse, n_sub):
    L = B_CHUNK
    W = B_PAIR_W
    R = 2 * L

    @pl.when(first_ref[pl.program_id(0)] == 1)
    def _():
        s_sc[...] = s0_ref[...]

    lane = lax.broadcasted_iota(jnp.int32, (L, W), 1)
    head0 = lane < B_HEAD_DIM
    ri = lax.broadcasted_iota(jnp.int32, (R, R), 0)
    ci = lax.broadcasted_iota(jnp.int32, (R, R), 1)
    same_head = (ri >= L) == (ci >= L)
    strict = same_head & ((ri < ci) if reverse else (ri > ci))
    incl = same_head & ((ri <= ci) if reverse else (ri >= ci))
    eye = (ri == ci).astype(F32)
    wi = lax.broadcasted_iota(jnp.int32, (W, W), 0)
    wj = lax.broadcasted_iota(jnp.int32, (W, W), 1)
    bd = (wi >= B_HEAD_DIM) == (wj >= B_HEAD_DIM)
    bd_ones = jnp.where(bd, 1.0, 0.0).astype(BF16)
    ti = lax.broadcasted_iota(jnp.int32, (L, L), 0)
    tj = lax.broadcasted_iota(jnp.int32, (L, L), 1)
    tri = jnp.where((tj >= ti) if reverse else (tj <= ti), 1.0, 0.0).astype(BF16)

    def stack_masked(x):
        return jnp.concatenate([jnp.where(head0, x, 0.0), jnp.where(head0, 0.0, x)], axis=0)

    def stack_dup(x):
        return jnp.concatenate([x, x], axis=0)

    subs = list(range(n_sub - 1, -1, -1) if reverse else range(n_sub))
    units = [(sub * L, slice(p * W, (p + 1) * W)) for sub in subs for p in range(B_PAIRS)]
    nu = len(units)

    def each(fn, *lists):
        return [fn(*args) for args in zip(*lists)]

    r_l = [r_ref[r0:r0 + L, sl].astype(F32) for r0, sl in units]
    k_l = [k_ref[r0:r0 + L, sl].astype(F32) for r0, sl in units]
    v_l = [v_ref[r0:r0 + L, sl].astype(F32) for r0, sl in units]
    lw_l = [-B_DECAY_SCALE * jax.nn.sigmoid(lw_ref[r0:r0 + L, sl]) for r0, sl in units]
    a_l = [jax.nn.sigmoid(a_ref[r0:r0 + L, sl]) for r0, sl in units]
    kkr_l = [k * kk_ref[:, sl] for k, (_, sl) in zip(k_l, units)]
    kd_l = [k * (1.0 + (a - 1.0) * ka_ref[:, sl]) for k, a, (_, sl) in zip(k_l, a_l, units)]
    sum_in = []
    for kkr, r, kd, (_, sl) in zip(kkr_l, r_l, kd_l, units):
        sum_in += _bf16_terms(kkr * kkr, 2) + _bf16_terms(r * kd * rk_ref[:, sl], 2)
    sums = _mxu(jnp.concatenate(sum_in, axis=0), bd_ones)
    ss_l = [sums[(4 * i) * L:(4 * i + 1) * L] + sums[(4 * i + 1) * L:(4 * i + 2) * L] for i in range(nu)]
    bsum_l = [sums[(4 * i + 2) * L:(4 * i + 3) * L] + sums[(4 * i + 3) * L:(4 * i + 4) * L] for i in range(nu)]
    lw_terms = []
    for lw in lw_l:
        lw_terms += _bf16_terms(lw, 2)
    csum = _mxu(tri, jnp.concatenate(lw_terms, axis=1))
    c_inc_l = [csum[:, (2 * i) * W:(2 * i + 1) * W] + csum[:, (2 * i + 1) * W:(2 * i + 2) * W] for i in range(nu)]
    kk_l = each(lambda kkr, ss: kkr / jnp.maximum(jnp.sqrt(ss), 1e-12), kkr_l, ss_l)
    bv_l = each(lambda kk, a: kk * a, kk_l, a_l)
    bonus_l = each(lambda bs, v: bs * v, bsum_l, v_l)
    c_mid_l = [c[L // 2:L // 2 + 1, :] for c in c_inc_l]
    tot_l = [(c[0:1, :] if reverse else c[L - 1:L, :]) for c in c_inc_l]
    x_all_l = each(lambda kk, r, c, lw: jnp.concatenate(
        [stack_masked(-kk * jnp.exp(c - lw)), stack_masked(r * jnp.exp(c))], axis=0), kk_l, r_l, c_inc_l, lw_l)
    xa_l = each(lambda x, cm: (x * jnp.exp(-cm)).astype(BF16), x_all_l, c_mid_l)
    x_all_l = [x.astype(BF16) for x in x_all_l]

    def decayed(bv, kd, e):
        return jnp.concatenate([stack_dup(bv * e), stack_dup(kd * e)], axis=0).astype(BF16)

    y_all_l = each(lambda bv, kd, c, cm: decayed(bv, kd, jnp.exp(cm - c)), bv_l, kd_l, c_inc_l, c_mid_l)
    y_end_l = each(lambda bv, kd, c, t: decayed(bv, kd, jnp.exp(t - c)), bv_l, kd_l, c_inc_l, tot_l)
    v_s_l = [stack_masked(v).astype(BF16) for v in v_l]

    a_all_l = each(lambda x, y: _mxu(x, y, _NT), xa_l, y_all_l)
    a_ab_l = [jnp.where(strict, a[:R, :R], 0.0) for a in a_all_l]
    a_ak_l = [jnp.where(strict, a[:R, R:], 0.0).astype(BF16) for a in a_all_l]
    a_r_l = [jnp.concatenate([jnp.where(incl, a[R:, :R], 0.0), jnp.where(incl, a[R:, R:], 0.0)],
                             axis=1).astype(BF16) for a in a_all_l]
    akv_l = each(lambda a, v: _mxu(a, v, _NN), a_ak_l, v_s_l)

    t_inv_l = [eye + a for a in a_ab_l]
    pw_l = [_mxu(a, a, _NN) for a in a_ab_l]
    for _ in range(4):
        both_l = each(lambda pw, t: _mxu(pw, jnp.concatenate([pw, t], axis=1), _NN), pw_l, t_inv_l)
        pw_l = [b[:, :R] for b in both_l]
        t_inv_l = each(lambda t, b: t + b[:, R:], t_inv_l, both_l)
    t_inv_l = each(lambda t, pw: (t + _mxu(pw, t, _NN)).astype(BF16), t_inv_l, pw_l)

    for i0 in range(0, nu, B_PAIRS):
        idx = list(range(i0, i0 + B_PAIRS))
        s_prev_l = [s_sc[p] for p in range(B_PAIRS)]
        xs_l = [_mxu(x_all_l[i], s, _NT) for i, s in zip(idx, s_prev_l)]
        u_l = [_mxu(t_inv_l[i], xs[:R] + akv_l[i], _NN) for i, xs in zip(idx, xs_l)]
        uv_l = [jnp.concatenate([u.astype(BF16), v_s_l[i]], axis=0) for i, u in zip(idx, u_l)]
        y_s_l = [xs[R:] + _mxu(a_r_l[i], uv, _NN) for i, xs, uv in zip(idx, xs_l, uv_l)]
        for i, y_s in zip(idx, y_s_l):
            r0, sl = units[i]
            y_ref[r0:r0 + L, sl] = y_s[:L] + y_s[L:] + bonus_l[i]
        s_new_l = [s * jnp.exp(tot_l[i]) + _mxu(uv, y_end_l[i], _TN)
                   for i, s, uv in zip(idx, s_prev_l, uv_l)]
        for p, s_new in enumerate(s_new_l):
            s_sc[p] = jnp.where(bd, s_new, 0.0)

    sf_ref[...] = s_sc[...]


def _rwkv_scan(r, k, v, lw, a, k_k, k_a, r_k, s0, *, seq_lens, reverse):
    cb = B_SCAN_ROWS
    nseq = len(seq_lens)
    rows = sum(seq_lens)
    blk, seq, first = _scan_schedule(seq_lens, cb, reverse)
    tile = pl.BlockSpec((cb, D_MODEL), lambda t, blk, seq, first: (blk[t], 0))
    vec = pl.BlockSpec((1, D_MODEL), lambda t, blk, seq, first: (0, 0))
    st = pl.BlockSpec((None, B_PAIRS, B_PAIR_W, B_PAIR_W), lambda t, blk, seq, first: (seq[t], 0, 0, 0))
    grid_spec = pltpu.PrefetchScalarGridSpec(
        num_scalar_prefetch=3,
        grid=(blk.shape[0],),
        in_specs=[tile, tile, tile, tile, tile, vec, vec, vec, st],
        out_specs=[tile, st],
        scratch_shapes=[pltpu.VMEM((B_PAIRS, B_PAIR_W, B_PAIR_W), F32)],
    )
    kern = functools.partial(_rwkv_scan_kernel, reverse=reverse, n_sub=cb // B_CHUNK)
    return pl.pallas_call(
        kern,
        grid_spec=grid_spec,
        out_shape=[
            jax.ShapeDtypeStruct((rows, D_MODEL), F32),
            jax.ShapeDtypeStruct((nseq, B_PAIRS, B_PAIR_W, B_PAIR_W), F32),
        ],
        compiler_params=_cparams("arbitrary"),
        name="rwkv_scan_bwd" if reverse else "rwkv_scan_fwd",
    )(blk, seq, first, r, k, v, lw, a, k_k, k_a, r_k, s0)


def _pair_blockdiag(s):
    n = s.shape[0]
    s = s.reshape(n, B_PAIRS, 2, B_HEAD_DIM, B_HEAD_DIM)
    z = jnp.zeros_like(s[:, :, 0])
    top = jnp.concatenate([s[:, :, 0], z], axis=-1)
    bot = jnp.concatenate([z, s[:, :, 1]], axis=-1)
    return jnp.concatenate([top, bot], axis=-2)


def _pair_unblock(sp):
    n = sp.shape[0]
    h0 = sp[:, :, :B_HEAD_DIM, :B_HEAD_DIM]
    h1 = sp[:, :, B_HEAD_DIM:, B_HEAD_DIM:]
    return jnp.stack([h0, h1], axis=2).reshape(n, B_HEADS, B_HEAD_DIM, B_HEAD_DIM)


TM = 256
TM_WIDE = 512
TM_MOE = 1024
MOE_SUB = 512
MOE_CAP = 160


def _mod_spec(layer, tm):
    per_group = GROUP_ROWS // tm
    return pl.BlockSpec((None, None, N_ADA, D_MODEL), lambda i, *_: (layer, i // per_group, 0, 0))


def _row_spec(tm, cols, col_block=0):
    return pl.BlockSpec((tm, cols), lambda i, *_: (i, col_block))


def _full_spec(shape):
    nd = len(shape)
    return pl.BlockSpec(shape, lambda i, *_: (0,) * nd)


def _split_specs(tm, cols, n_first):
    t_first = n_first // tm
    return (pl.BlockSpec((tm, cols), lambda i, *_: (jnp.minimum(i, t_first - 1), 0)),
            pl.BlockSpec((tm, cols), lambda i, *_: (jnp.maximum(i - t_first, 0), 0)))


def _in_proj_kernel(xp_ref, xs_ref, mod_ref, g_ref, w_ref, b_ref, wg_ref, bg_ref, z_ref, gate_ref, *, ctx_tiles):
    x = jnp.where(pl.program_id(0) < ctx_tiles, xp_ref[...], xs_ref[...])
    hn = _norm_mod(x, g_ref[...], mod_ref[1:2, :], mod_ref[0:1, :])
    z_ref[...] = _dot(hn.astype(BF16), w_ref[...]) + b_ref[...]
    gate_ref[...] = _mxu(hn, wg_ref[...], na=2, nb=2) + bg_ref[...]


def _in_proj(xp, xs, mod, g1, w_main, b_main, w_gate, b_gate):
    rows = xp.shape[0] + xs.shape[0]
    cols = w_main.shape[1]
    tm = TM_WIDE
    return pl.pallas_call(
        functools.partial(_in_proj_kernel, ctx_tiles=xp.shape[0] // tm),
        grid=(rows // tm,),
        in_specs=[
            *_split_specs(tm, D_MODEL, xp.shape[0]), _mod_spec(0, tm), _full_spec((1, D_MODEL)),
            _full_spec(w_main.shape), _full_spec(b_main.shape), _full_spec(w_gate.shape), _full_spec(b_gate.shape),
        ],
        out_specs=[_row_spec(tm, cols), _row_spec(tm, A_GATE_COLS)],
        out_shape=[jax.ShapeDtypeStruct((rows, cols), F32), jax.ShapeDtypeStruct((rows, A_GATE_COLS), F32)],
        compiler_params=_cparams("parallel"),
        name="mlstm_in_proj",
    )(xp, xs, mod, g1, w_main, b_main, w_gate, b_gate)


def _mlstm_out_ff_kernel(hf_ref, hb_ref, o_ref, xp_ref, xs_ref, mod_ref, modn_ref, gh_ref, wo_ref, g2_ref,
                         wgu_ref, wdn_ref, gn_ref, x_out_ref, hn_out_ref, *, ctx_tiles):
    x = jnp.where(pl.program_id(0) < ctx_tiles, xp_ref[...], xs_ref[...])
    hs = hf_ref[...] + hb_ref[...]
    parts = []
    for h in range(A_HEADS):
        seg = hs[:, h * A_DV:(h + 1) * A_DV]
        parts.append(seg * lax.rsqrt(jnp.mean(seg * seg, axis=-1, keepdims=True) + RMS_EPS))
    y = (jnp.concatenate(parts, axis=1) * gh_ref[...]) * jax.nn.sigmoid(o_ref[...])
    x1 = x + mod_ref[2:3, :] * _dot(y.astype(BF16), wo_ref[...])
    hn2 = _norm_mod(x1, g2_ref[...], mod_ref[4:5, :], mod_ref[3:4, :])
    gu = _dot(hn2.astype(BF16), wgu_ref[...])
    act = jax.nn.silu(gu[:, :D_FF]) * gu[:, D_FF:]
    x2 = x1 + mod_ref[5:6, :] * _dot(act.astype(BF16), wdn_ref[...])
    x_out_ref[...] = x2
    hn_out_ref[...] = _norm_mod(x2, gn_ref[...], modn_ref[1:2, :], modn_ref[0:1, :])


def _mlstm_out_ff(hf, hb, z, xp, xs, mod, g_head, w_out, g2, w_gu, w_down, g1_next):
    rows = xp.shape[0] + xs.shape[0]
    return pl.pallas_call(
        functools.partial(_mlstm_out_ff_kernel, ctx_tiles=xp.shape[0] // TM),
        grid=(rows // TM,),
        in_specs=[
            _row_spec(TM, A_V), _row_spec(TM, A_V), _row_spec(TM, A_V, 2), *_split_specs(TM, D_MODEL, xp.shape[0]),
            _mod_spec(0, TM), _mod_spec(1, TM), _full_spec((1, A_V)), _full_spec(w_out.shape),
            _full_spec((1, D_MODEL)), _full_spec(w_gu.shape), _full_spec(w_down.shape), _full_spec((1, D_MODEL)),
        ],
        out_specs=[_row_spec(TM, D_MODEL), _row_spec(TM, D_MODEL)],
        out_shape=[jax.ShapeDtypeStruct((rows, D_MODEL), F32)] * 2,
        compiler_params=_cparams("parallel"),
        name="mlstm_out_ff",
    )(hf, hb, z, xp, xs, mod, mod, g_head, w_out, g2, w_gu, w_down, g1_next)


def _token_shift(cur, up, dn, is_ctx, first, last):
    tm = cur.shape[0]
    q = D_MODEL // 4
    row = lax.broadcasted_iota(jnp.int32, (tm, q), 0)
    col = row % GRID_W
    prev = pltpu.roll(cur, 1, 0)
    nxt = pltpu.roll(cur, tm - 1, 0)
    q0 = jnp.where(jnp.where(is_ctx, row, col) == 0, 0.0, prev[:, :q])
    q1 = jnp.where(is_ctx, jnp.where(row == 0, 0.0, prev[:, q:2 * q]),
                   jnp.where(col == GRID_W - 1, 0.0, nxt[:, q:2 * q]))
    upper = jnp.concatenate([up[:, 2 * q:3 * q], cur[:tm - GRID_W, 2 * q:3 * q]], axis=0)
    upper = jnp.where(first & (row < GRID_W), 0.0, upper)
    q2 = jnp.where(is_ctx, jnp.where(row == tm - 1, 0.0, nxt[:, 2 * q:3 * q]), upper)
    lower = jnp.concatenate([cur[GRID_W:, 3 * q:], dn[:, 3 * q:]], axis=0)
    lower = jnp.where(last & (row >= tm - GRID_W), 0.0, lower)
    q3 = jnp.where(is_ctx, jnp.where(row == tm - 1, 0.0, nxt[:, 3 * q:]), lower)
    return jnp.concatenate([q0, q1, q2, q3], axis=1)


def _rwkv_in_kernel(hn_ref, up_ref, dn_ref, mu_ref, wrkv_ref, w1_ref, w2_ref, w0_ref, a1_ref, a2_ref, a0_ref,
                    g1_ref, g2_ref, r_ref, k_ref, v_ref, lw0_ref, lw1_ref, ia0_ref, ia1_ref, g_ref,
                    *, ctx_tiles, tiles_per_seq):
    i = pl.program_id(0)
    j = (i - ctx_tiles) % tiles_per_seq
    h = hn_ref[...]
    xx = _token_shift(h, up_ref[...], dn_ref[...], i < ctx_tiles, j == 0, j == tiles_per_seq - 1) - h

    def mix(c):
        return (h + xx * mu_ref[c:c + 1, :]).astype(BF16)

    r_ref[...] = _dot(mix(0), wrkv_ref[0]).astype(BF16)
    k_ref[...] = _dot(mix(1), wrkv_ref[1]).astype(BF16)
    v_ref[...] = _dot(mix(2), wrkv_ref[2]).astype(BF16)
    lane = lax.broadcasted_iota(jnp.int32, (h.shape[0], 2 * 64), 1)
    dir0 = lane < 64
    t = jnp.tanh(_dot(mix(3), w1_ref[...]))
    lw0_ref[...] = w0_ref[0:1, :] + _dot(jnp.where(dir0, t, 0.0).astype(BF16), w2_ref[...])
    lw1_ref[...] = w0_ref[1:2, :] + _dot(jnp.where(dir0, 0.0, t).astype(BF16), w2_ref[...])
    u = _dot(mix(4), a1_ref[...])
    ia0_ref[...] = a0_ref[0:1, :] + _dot(jnp.where(dir0, u, 0.0).astype(BF16), a2_ref[...])
    ia1_ref[...] = a0_ref[1:2, :] + _dot(jnp.where(dir0, 0.0, u).astype(BF16), a2_ref[...])
    g_ref[...] = _dot(jax.nn.sigmoid(_dot(mix(5), g1_ref[...])).astype(BF16), g2_ref[...]).astype(BF16)


def _rwkv_in(hn, mu, w_rkv, w1, w2, w0, a1, a2, a0, g1, g2, *, n_ctx, ctx_len, lat_len):
    rows = hn.shape[0]
    assert ctx_len == TM and lat_len % TM == 0 and TM % GRID_W == 0
    per_tile = TM // GRID_W
    n_halo = rows // GRID_W
    ins = [mu, w_rkv, w1, w2, w0, a1, a2, a0, g1, g2]
    kern = functools.partial(_rwkv_in_kernel, ctx_tiles=n_ctx // TM, tiles_per_seq=lat_len // TM)
    return pl.pallas_call(
        kern,
        grid=(rows // TM,),
        in_specs=[
            _row_spec(TM, D_MODEL),
            pl.BlockSpec((GRID_W, D_MODEL), lambda i: (jnp.maximum(i * per_tile - 1, 0), 0)),
            pl.BlockSpec((GRID_W, D_MODEL), lambda i: (jnp.minimum((i + 1) * per_tile, n_halo - 1), 0)),
        ] + [_full_spec(a.shape) for a in ins],
        out_specs=[_row_spec(TM, D_MODEL)] * 8,
        out_shape=[jax.ShapeDtypeStruct((rows, D_MODEL), dt) for dt in (BF16, BF16, BF16, F32, F32, F32, F32, BF16)],
        compiler_params=_cparams("parallel"),
        name="rwkv_in_proj",
    )(hn, hn, hn, *ins)


def _rwkv_out_kernel(yf_ref, yb_ref, g_ref, x_ref, mod_ref, lng_ref, lnb_ref, wo_ref, x_out_ref):
    y = yf_ref[...] + yb_ref[...]
    wi = lax.broadcasted_iota(jnp.int32, (B_PAIR_W, B_PAIR_W), 0)
    wj = lax.broadcasted_iota(jnp.int32, (B_PAIR_W, B_PAIR_W), 1)
    bd_mean = jnp.where((wi >= B_HEAD_DIM) == (wj >= B_HEAD_DIM), 1.0 / B_HEAD_DIM, 0.0)
    parts = []
    for p in range(B_PAIRS):
        seg = y[:, p * B_PAIR_W:(p + 1) * B_PAIR_W]
        cen = seg - _mxu(seg, bd_mean, na=3)
        var = _mxu(cen * cen, bd_mean, na=3)
        parts.append(cen * lax.rsqrt(var + B_LN_EPS))
    yn = jnp.concatenate(parts, axis=1) * lng_ref[...] + lnb_ref[...]
    out = _dot((yn * g_ref[...]).astype(BF16), wo_ref[...])
    x_out_ref[...] = x_ref[...] + mod_ref[2:3, :] * out


def _rwkv_out(yf, yb, g, x, mod, ln_g, ln_b, w_out):
    rows = x.shape[0]
    tm = TM_WIDE
    return pl.pallas_call(
        _rwkv_out_kernel,
        grid=(rows // tm,),
        in_specs=[_row_spec(tm, D_MODEL)] * 4 + [
            _mod_spec(1, tm), _full_spec((1, D_MODEL)), _full_spec((1, D_MODEL)), _full_spec(w_out.shape)],
        out_specs=_row_spec(tm, D_MODEL),
        out_shape=jax.ShapeDtypeStruct((rows, D_MODEL), F32),
        compiler_params=_cparams("parallel"),
        name="rwkv_out_proj",
    )(yf, yb, g, x, mod, ln_g, ln_b, w_out)


def _moe_kernel(x_ref, mod_ref, g2_ref, wr_ref, br_ref, upper_ref, wgu_ref, wdn_ref, gf_ref, out_p_ref, out_s_ref,
                hn_sc, sel_sc, cw_sc, rank_sc, acc_sc, *, ctx_tiles):
    e = pl.program_id(1)
    tm = x_ref.shape[0]
    n_sub = tm // MOE_SUB

    @pl.when(e == 0)
    def _():
        hn = _norm_mod(x_ref[...], g2_ref[...], mod_ref[4:5, :], mod_ref[3:4, :])
        hn_sc[...] = hn.astype(BF16)
        logits = _mxu(hn, wr_ref[...], na=3, nb=3) + br_ref[...]
        lane = lax.broadcasted_iota(jnp.int32, logits.shape, 1)
        m1 = jnp.max(logits, axis=1, keepdims=True)
        i1 = jnp.min(jnp.where(logits == m1, lane, ROUTER_COLS), axis=1, keepdims=True)
        rest = jnp.where(lane == i1, -jnp.inf, logits)
        m2 = jnp.max(rest, axis=1, keepdims=True)
        i2 = jnp.min(jnp.where(rest == m2, lane, ROUTER_COLS), axis=1, keepdims=True)
        e2 = jnp.exp(m2 - m1)
        den = 1.0 + e2
        comb = jnp.where(lane == i1, 1.0 / den, 0.0) + jnp.where(lane == i2, e2 / den, 0.0)
        sel = jnp.where((lane == i1) | (lane == i2), 1.0, 0.0)
        for s in range(n_sub):
            rows = slice(s * MOE_SUB, (s + 1) * MOE_SUB)
            sel_t = sel[rows, :].T[:N_EXPERTS, :]
            sel_sc[s] = sel_t
            cw_sc[s] = comb[rows, :].T[:N_EXPERTS, :]
            rank_sc[s] = _mxu(sel_t, upper_ref[...]).astype(jnp.int32)
        acc_sc[...] = jnp.zeros_like(acc_sc)

    for s in range(n_sub):
        rows = slice(s * MOE_SUB, (s + 1) * MOE_SUB)
        sel_row = sel_sc[s, pl.ds(e, 1), :]
        cw_row = cw_sc[s, pl.ds(e, 1), :]
        rank_row = rank_sc[s, pl.ds(e, 1), :]
        count = jnp.sum(sel_row).astype(jnp.int32)

        def evaluate(first_rank, cap):
            slot = lax.broadcasted_iota(jnp.int32, (cap, MOE_SUB), 0) + first_rank
            pick = jnp.where((rank_row == slot) & (sel_row > 0.5), 1.0, 0.0)
            pick_bf = pick.astype(BF16)
            xg = _mxu(pick_bf, hn_sc[rows, :])
            gu = _mxu(xg, wgu_ref[...])
            act = jax.nn.silu(gu[:, :D_FF_EXPERT]) * gu[:, D_FF_EXPERT:]
            ye = _mxu(act, wdn_ref[...])
            wj = jnp.sum(pick * cw_row, axis=1, keepdims=True)
            acc_sc[rows, :] += _mxu(pick_bf, wj * ye, _TN)

        def one_round(c, carry):
            evaluate(c * MOE_CAP, MOE_CAP)
            return carry

        lax.fori_loop(0, (count + MOE_CAP - 1) // MOE_CAP, one_round, 0)

    def final_norm():
        x4 = x_ref[...] + mod_ref[5:6, :] * acc_sc[...]
        return (x4 * lax.rsqrt(jnp.mean(x4 * x4, axis=-1, keepdims=True) + RMS_EPS)) * gf_ref[...]

    is_ctx = pl.program_id(0) < ctx_tiles

    @pl.when((e == N_EXPERTS - 1) & is_ctx)
    def _():
        out_p_ref[...] = final_norm()

    @pl.when((e == N_EXPERTS - 1) & jnp.logical_not(is_ctx))
    def _():
        out_s_ref[...] = final_norm()


def _moe(x, mod, g2, w_router, b_router, w_gu, w_down, g_final, *, n_ctx):
    rows = x.shape[0]
    tm = TM_MOE
    n_sub = tm // MOE_SUB
    upper = jnp.triu(jnp.ones((MOE_SUB, MOE_SUB), BF16), k=1)
    return pl.pallas_call(
        functools.partial(_moe_kernel, ctx_tiles=n_ctx // tm),
        grid=(rows // tm, N_EXPERTS),
        in_specs=[
            _row_spec(tm, D_MODEL), _mod_spec(1, tm), _full_spec((1, D_MODEL)),
            _full_spec(w_router.shape), _full_spec(b_router.shape), _full_spec(upper.shape),
            pl.BlockSpec((None, D_MODEL, 2 * D_FF_EXPERT), lambda i, e: (e, 0, 0)),
            pl.BlockSpec((None, D_FF_EXPERT, D_MODEL), lambda i, e: (e, 0, 0)),
            _full_spec((1, D_MODEL)),
        ],
        out_specs=list(_split_specs(tm, D_MODEL, n_ctx)),
        out_shape=[jax.ShapeDtypeStruct((n_ctx, D_MODEL), F32), jax.ShapeDtypeStruct((rows - n_ctx, D_MODEL), F32)],
        scratch_shapes=[
            pltpu.VMEM((tm, D_MODEL), BF16),
            pltpu.VMEM((n_sub, N_EXPERTS, MOE_SUB), F32),
            pltpu.VMEM((n_sub, N_EXPERTS, MOE_SUB), F32),
            pltpu.VMEM((n_sub, N_EXPERTS, MOE_SUB), jnp.int32),
            pltpu.VMEM((tm, D_MODEL), F32),
        ],
        compiler_params=_cparams("arbitrary", "arbitrary"),
        name="moe_final",
    )(x, mod, g2, w_router, b_router, upper, w_gu, w_down, g_final)


def _pad_cols(a, cols, value=0.0):
    return jnp.pad(a, ((0, 0), (0, cols - a.shape[1])), constant_values=value)


def kernel(x_prompt, x_sample, state_mlstm_C, state_mlstm_n, state_mlstm_m, state_rwkv_S, c, c_ctx, w_ada, b_ada, g_norm1, g_norm2, w_m_in, b_m_in, g_m_head, w_m_out, mu_rwkv, w_rkv, w_decay0, w_decay1, w_decay2, w_iclr0, w_iclr1, w_iclr2, w_gate1, w_gate2, k_k, k_a, r_k, g_ln_x, b_ln_x, w_rwkv_out, w_ff_gu, w_ff_down, w_router, b_router, w_exp_gu, w_exp_down, g_final):
    nb, seq, d = x_prompt.shape
    db, dseq, _ = x_sample.shape
    n_ctx = nb * seq
    n_lat = db * dseq
    assert d == D_MODEL and n_ctx == GROUP_ROWS and dseq == GROUP_ROWS

    xp = x_prompt.reshape(n_ctx, d)
    xs = x_sample.reshape(n_lat, d)
    cond = jnp.concatenate([c_ctx[None, :], c], axis=0)
    n_groups = cond.shape[0]
    cond = jnp.pad(cond, ((0, (-n_groups) % 8), (0, 0)))
    mod = _ada(cond, w_ada, b_ada).reshape(w_ada.shape[0], cond.shape[0], N_ADA, d)
    row = lambda a: a.reshape(1, -1)

    n_main = 2 * A_QK + 2 * A_V
    w_in = w_m_in[0]
    z, gates = _in_proj(xp, xs, mod, row(g_norm1[0]), w_in[:, :n_main].astype(BF16), row(b_m_in[0][:n_main]),
                        _pad_cols(w_in[:, n_main:], A_GATE_COLS), _pad_cols(row(b_m_in[0][n_main:]), A_GATE_COLS))
    gates_t = gates[:, :4 * A_HEADS].T
    seq_lens = (seq,) * nb + (dseq,) * db
    with_zero_ctx = lambda s: jnp.concatenate([jnp.zeros((nb,) + s.shape[1:], F32), s.astype(F32)], axis=0)
    h_dirs, ctx_state = [], []
    for dirn, rev in ((0, False), (1, True)):
        h_d, cf, nf, mf = _mlstm_scan(
            z, gates, gates_t, with_zero_ctx(state_mlstm_C[:, 0, dirn]), with_zero_ctx(state_mlstm_n[:, 0, dirn]),
            with_zero_ctx(jnp.broadcast_to(state_mlstm_m[:, 0, dirn][..., None], (db, A_HEADS, A_DV))),
            seq_lens=seq_lens, reverse=rev)
        h_dirs.append(h_d)
        ctx_state.append((cf[:nb], nf[:nb], mf[:nb, :, 0]))
    new_c = jnp.stack([s[0] for s in ctx_state], axis=1)[:, None]
    new_n = jnp.stack([s[1] for s in ctx_state], axis=1)[:, None]
    new_m = jnp.stack([s[2] for s in ctx_state], axis=1)[:, None]
    x, hn = _mlstm_out_ff(h_dirs[0], h_dirs[1], z, xp, xs, mod, row(g_m_head[0]), w_m_out[0].astype(BF16),
                          row(g_norm2[0]), w_ff_gu[0].astype(BF16), w_ff_down[0].astype(BF16), row(g_norm1[1]))

    cat_cols = lambda w: jnp.concatenate([w[0], w[1]], axis=1).astype(BF16)
    cat_rows = lambda w: jnp.concatenate([w[0], w[1]], axis=0).astype(BF16)
    r, k, v, lw0, lw1, ia0, ia1, g = _rwkv_in(
        hn, mu_rwkv[0], w_rkv[0].astype(BF16), cat_cols(w_decay1[0]), cat_rows(w_decay2[0]), w_decay0[0],
        cat_cols(w_iclr1[0]), cat_rows(w_iclr2[0]), w_iclr0[0], w_gate1[0].astype(BF16), w_gate2[0].astype(BF16),
        n_ctx=n_ctx, ctx_len=seq, lat_len=dseq)
    y_dirs, ctx_s = [], []
    for dirn, rev, lw, ia in ((0, False, lw0, ia0), (1, True, lw1, ia1)):
        y_d, sf = _rwkv_scan(r, k, v, lw, ia, row(k_k[0]), row(k_a[0]), row(r_k[0]),
                             with_zero_ctx(_pair_blockdiag(state_rwkv_S[:, 0, dirn])), seq_lens=seq_lens, reverse=rev)
        y_dirs.append(y_d)
        ctx_s.append(_pair_unblock(sf[:nb]))
    new_s = jnp.stack(ctx_s, axis=1)[:, None]
    x = _rwkv_out(y_dirs[0], y_dirs[1], g, x, mod, row(g_ln_x[0]), row(b_ln_x[0]), w_rwkv_out[0].astype(BF16))
    y_p, y_s = _moe(x, mod, row(g_norm2[1]), _pad_cols(w_router[0], ROUTER_COLS),
                    _pad_cols(row(b_router[0]), ROUTER_COLS, NEG_BIG), w_exp_gu[0].astype(BF16),
                    w_exp_down[0].astype(BF16), row(g_final), n_ctx=n_ctx)
    return (y_p.reshape(nb, seq, d), y_s.reshape(db, dseq, d), new_c, new_n, new_m, new_s)
```

```python
import functools

import jax
import jax.numpy as jnp
from jax import lax
from jax.experimental import pallas as pl
from jax.experimental.pallas import tpu as pltpu

F32 = jnp.float32
BF16 = jnp.bfloat16
HIGHEST = lax.Precision.HIGHEST

D_MODEL = 1024
GROUP_ROWS = 4096
N_ADA = 6
RMS_EPS = 1e-6
GRID_W = 64
A_HEADS = 8
A_DK = 64
A_DV = 128
A_QK = A_HEADS * A_DK
A_V = A_HEADS * A_DV
A_CHUNK = 128
A_GATE_COLS = 128
A_SCAN_ROWS = 256
B_HEADS = 16
B_HEAD_DIM = 64
B_PAIRS = B_HEADS // 2
B_PAIR_W = 2 * B_HEAD_DIM
B_CHUNK = 64
B_DECAY_SCALE = 0.606531
B_LN_EPS = 64e-5
B_SCAN_ROWS = 256
D_FF = 2816
N_EXPERTS = 8
D_FF_EXPERT = D_FF // 2
ROUTER_COLS = 128
NEG_BIG = -1e30

VMEM_LIMIT = 56 * 1024 * 1024


def _cparams(*sem):
    return pltpu.CompilerParams(dimension_semantics=sem, vmem_limit_bytes=VMEM_LIMIT)


def _dot(a, b, precision=None):
    return jnp.dot(a, b, precision=precision, preferred_element_type=F32)


def _dot_nt(a, b, precision=None):
    return lax.dot_general(a, b, (((1,), (1,)), ((), ())), precision=precision, preferred_element_type=F32)


def _dot_tn(a, b, precision=None):
    return lax.dot_general(a, b, (((0,), (0,)), ((), ())), precision=precision, preferred_element_type=F32)


_NN = (((1,), (0,)), ((), ()))
_NT = (((1,), (1,)), ((), ()))
_TN = (((0,), (0,)), ((), ()))


def _bf16_terms(a, n):
    terms = []
    for i in range(n):
        t = a.astype(BF16)
        terms.append(t)
        if i + 1 < n:
            a = a - t.astype(F32)
    return terms


def _mxu(a, b, dims=_NN, na=1, nb=1):
    at = _bf16_terms(a, na)
    bt = _bf16_terms(b, nb)
    out = None
    for i, x in enumerate(at):
        for j, y in enumerate(bt):
            if i + j < max(na, nb):
                t = lax.dot_general(x, y, dims, preferred_element_type=F32)
                out = t if out is None else out + t
    return out


def _norm_mod(x, g, sc, sh):
    y = x * lax.rsqrt(jnp.mean(x * x, axis=-1, keepdims=True) + RMS_EPS)
    return (y * g) * (1.0 + sc) + sh


def _log_sigmoid(x):
    return jnp.minimum(x, 0.0) - jnp.log1p(jnp.exp(-jnp.abs(x)))


def _ada_kernel(cond_ref, w_ref, b_ref, out_ref):
    c = cond_ref[...]
    out_ref[...] = _dot(c * jax.nn.sigmoid(c), w_ref[...], HIGHEST) + b_ref[...]


def _ada(cond_pad, w_ada, b_ada):
    depth, d, cols = w_ada.shape
    rows = cond_pad.shape[0]
    tn = 1536
    return pl.pallas_call(
        _ada_kernel,
        grid=(depth, cols // tn),
        in_specs=[
            pl.BlockSpec((rows, d), lambda l, j: (0, 0)),
            pl.BlockSpec((None, d, tn), lambda l, j: (l, 0, j)),
            pl.BlockSpec((None, 1, tn), lambda l, j: (l, 0, j)),
        ],
        out_specs=pl.BlockSpec((None, rows, tn), lambda l, j: (l, 0, j)),
        out_shape=jax.ShapeDtypeStruct((depth, rows, cols), F32),
        compiler_params=_cparams("parallel", "parallel"),
        name="ada",
    )(cond_pad, w_ada, b_ada.reshape(depth, 1, cols))


def _mlstm_scan_kernel(blk_ref, seq_ref, first_ref, q_ref, k_ref, v_ref, gc_ref, gr_ref, c0_ref, n0_ref, m0_ref,
                       h_ref, cf_ref, nf_ref, mf_ref, c_sc, n_sc, m_sc, *, reverse, n_sub):
    L = A_CHUNK

    @pl.when(first_ref[pl.program_id(0)] == 1)
    def _():
        c_sc[...] = c0_ref[...]
        n_sc[...] = n0_ref[...]
        m_sc[...] = m0_ref[...]

    row = lax.broadcasted_iota(jnp.int32, (L, L), 0)
    col = lax.broadcasted_iota(jnp.int32, (L, L), 1)
    in_win = (col >= row) if reverse else (col <= row)
    in_win_t = (row >= col) if reverse else (row <= col)
    d0 = A_HEADS if reverse else 0

    subs = list(range(n_sub - 1, -1, -1) if reverse else range(n_sub))
    units = [(sub * L, h) for sub in subs for h in range(A_HEADS)]
    fcol = {sub: _log_sigmoid(gc_ref[sub * L:(sub + 1) * L, :]) for sub in subs}
    frow = {sub: _log_sigmoid(gr_ref[:, sub * L:(sub + 1) * L]) for sub in subs}

    ci_l = [d0 + h for _, h in units]
    cf_l = [2 * A_HEADS + d0 + h for _, h in units]
    i_col_l = [gc_ref[r0:r0 + L, ci:ci + 1] for (r0, _), ci in zip(units, ci_l)]
    i_row_l = [gr_ref[ci:ci + 1, r0:r0 + L] for (r0, _), ci in zip(units, ci_l)]
    f_col_l = [fcol[r0 // L][:, cf:cf + 1] for (r0, _), cf in zip(units, cf_l)]
    f_row_l = [frow[r0 // L][cf:cf + 1, :] for (r0, _), cf in zip(units, cf_l)]
    b_col_l = [jnp.sum(jnp.where(in_win, f, 0.0), axis=1, keepdims=True) for f in f_row_l]
    b_row_l = [jnp.sum(jnp.where(in_win_t, f, 0.0), axis=0, keepdims=True) for f in f_col_l]
    tot_l = [jnp.sum(f, axis=1, keepdims=True) for f in f_row_l]
    logw_l = [jnp.where(in_win, bc - br + ir, -jnp.inf) for bc, br, ir in zip(b_col_l, b_row_l, i_row_l)]
    mx_l = [jnp.max(lw, axis=1, keepdims=True) for lw in logw_l]
    p_l = [jnp.exp(lw - mx) for lw, mx in zip(logw_l, mx_l)]
    rep = lambda x: jnp.broadcast_to(x, (x.shape[0], A_DV))
    b_col_l = [rep(b) for b in b_col_l]
    mx_l = [rep(m) for m in mx_l]
    tot_l = [rep(t) for t in tot_l]
    log_g_l = [t - bc + rep(ic) for t, bc, ic in zip(tot_l, b_col_l, i_col_l)]
    mg_l = [jnp.max(lg, axis=0, keepdims=True) for lg in log_g_l]
    wg_l = [jnp.exp(lg - mg)[:, :A_DK] for lg, mg in zip(log_g_l, mg_l)]
    q_l = [q_ref[r0:r0 + L, h * A_DK:(h + 1) * A_DK] for r0, h in units]
    k_l = [k_ref[r0:r0 + L, h * A_DK:(h + 1) * A_DK] * (A_DK ** -0.5) for r0, h in units]
    v_l = [v_ref[r0:r0 + L, h * A_DV:(h + 1) * A_DV].astype(BF16) for r0, h in units]
    qk_l = [_mxu(q, k, _NT) for q, k in zip(q_l, k_l)]
    sp_l = [qk * p for qk, p in zip(qk_l, p_l)]
    ssum_l = [rep(jnp.sum(sp, axis=1, keepdims=True)) for sp in sp_l]
    sv_l = [_mxu(sp, v) for sp, v in zip(sp_l, v_l)]
    kw_l = [k * wg for k, wg in zip(k_l, wg_l)]
    ksum_l = [jnp.sum(kw, axis=0, keepdims=True) for kw in kw_l]
    kv_l = [_mxu(kw, v, _TN) for kw, v in zip(kw_l, v_l)]

    nu = len(units)
    heads = range(A_HEADS)
    m_cur = [m_sc[h:h + 1, :] for h in heads]
    c_cur = [c_sc[h] for h in heads]
    n_cur = [n_sc[h:h + 1, :] for h in heads]
    m_prev, c_prev, n_prev = [None] * nu, [None] * nu, [None] * nu
    for i0 in range(0, nu, A_HEADS):
        for h in heads:
            i = i0 + h
            m_prev[i], c_prev[i], n_prev[i] = m_cur[h], c_cur[h], n_cur[h]
            m_new = jnp.maximum(tot_l[i] + m_cur[h], mg_l[i])
            decay = jnp.exp(tot_l[i] + m_cur[h] - m_new)
            gain = jnp.exp(mg_l[i] - m_new)
            c_cur[h] = decay * c_cur[h] + gain * kv_l[i]
            n_cur[h] = decay[:, :A_DK] * n_cur[h] + gain[:, :A_DK] * ksum_l[i]
            m_cur[h] = m_new
    for h in heads:
        c_sc[h] = c_cur[h]
        n_sc[h:h + 1, :] = n_cur[h]
        m_sc[h:h + 1, :] = m_cur[h]

    qc = [_mxu(q, c) for q, c in zip(q_l, c_prev)]
    qn = [rep(jnp.sum(q * n, axis=1, keepdims=True)) for q, n in zip(q_l, n_prev)]
    inter = [b + m for b, m in zip(b_col_l, m_prev)]
    m_row = [jnp.maximum(it, mx) for it, mx in zip(inter, mx_l)]
    corr = [jnp.exp(mx - mr) for mx, mr in zip(mx_l, m_row)]
    w_inter = [jnp.exp(it - mr) for it, mr in zip(inter, m_row)]
    floor = [jnp.exp(-mr) for mr in m_row]
    num = [cr * sv + wi * x for cr, sv, wi, x in zip(corr, sv_l, w_inter, qc)]
    den = [cr * ss + wi * x for cr, ss, wi, x in zip(corr, ssum_l, w_inter, qn)]
    for (r0, h), nm, dn, fl in zip(units, num, den, floor):
        h_ref[r0:r0 + L, h * A_DV:(h + 1) * A_DV] = nm / jnp.maximum(jnp.abs(dn), fl)

    cf_ref[...] = c_sc[...]
    nf_ref[...] = n_sc[...]
    mf_ref[...] = m_sc[...]


def _scan_schedule(seq_lens, cb, reverse):
    blk, seq, first = [], [], []
    base = 0
    for s, n in enumerate(seq_lens):
        nb = n // cb
        order = range(nb - 1, -1, -1) if reverse else range(nb)
        for j, b in enumerate(order):
            blk.append(base + b)
            seq.append(s)
            first.append(1 if j == 0 else 0)
        base += nb
    as_i32 = lambda a: jnp.asarray(a, dtype=jnp.int32)
    return as_i32(blk), as_i32(seq), as_i32(first)


def _mlstm_scan(z, gates, gates_t, c0, n0, m0, *, seq_lens, reverse):
    cb = A_SCAN_ROWS
    nseq = len(seq_lens)
    rows = sum(seq_lens)
    blk, seq, first = _scan_schedule(seq_lens, cb, reverse)
    tile = lambda cols, col_block: pl.BlockSpec((cb, cols), lambda t, blk, seq, first: (blk[t], col_block))
    state = lambda *shape: pl.BlockSpec((None,) + shape, lambda t, blk, seq, first: (seq[t],) + (0,) * len(shape))
    grid_spec = pltpu.PrefetchScalarGridSpec(
        num_scalar_prefetch=3,
        grid=(blk.shape[0],),
        in_specs=[
            tile(A_QK, 0), tile(A_QK, 1), tile(A_V, 1), tile(A_GATE_COLS, 0),
            pl.BlockSpec((4 * A_HEADS, cb), lambda t, blk, seq, first: (0, blk[t])),
            state(A_HEADS, A_DK, A_DV), state(A_HEADS, A_DK), state(A_HEADS, A_DV),
        ],
        out_specs=[tile(A_V, 0), state(A_HEADS, A_DK, A_DV), state(A_HEADS, A_DK), state(A_HEADS, A_DV)],
        scratch_shapes=[
            pltpu.VMEM((A_HEADS, A_DK, A_DV), F32),
            pltpu.VMEM((A_HEADS, A_DK), F32),
            pltpu.VMEM((A_HEADS, A_DV), F32),
        ],
    )
    kern = functools.partial(_mlstm_scan_kernel, reverse=reverse, n_sub=cb // A_CHUNK)
    return pl.pallas_call(
        kern,
        grid_spec=grid_spec,
        out_shape=[
            jax.ShapeDtypeStruct((rows, A_V), F32),
            jax.ShapeDtypeStruct((nseq, A_HEADS, A_DK, A_DV), F32),
            jax.ShapeDtypeStruct((nseq, A_HEADS, A_DK), F32),
            jax.ShapeDtypeStruct((nseq, A_HEADS, A_DV), F32),
        ],
        compiler_params=_cparams("arbitrary"),
        name="mlstm_scan_bwd" if reverse else "mlstm_scan_fwd",
    )(blk, seq, first, z, z, z, gates, gates_t, c0, n0, m0)


def _rwkv_scan_kernel(blk_ref, seq_ref, first_ref, r_ref, k_ref, v_ref, lw_ref, a_ref, kk_ref, ka_ref, rk_ref,
                      s0_ref, y_ref, sf_ref, s_sc, *, reverse, n_sub):
    L = B_CHUNK
    W = B_PAIR_W
    R = 2 * L

    @pl.when(first_ref[pl.program_id(0)] == 1)
    def _():
        s_sc[...] = s0_ref[...]

    lane = lax.broadcasted_iota(jnp.int32, (L, W), 1)
    head0 = lane < B_HEAD_DIM
    ri = lax.broadcasted_iota(jnp.int32, (R, R), 0)
    ci = lax.broadcasted_iota(jnp.int32, (R, R), 1)
    same_head = (ri >= L) == (ci >= L)
    strict = same_head & ((ri < ci) if reverse else (ri > ci))
    incl = same_head & ((ri <= ci) if reverse else (ri >= ci))
    eye = (ri == ci).astype(F32)
    wi = lax.broadcasted_iota(jnp.int32, (W, W), 0)
    wj = lax.broadcasted_iota(jnp.int32, (W, W), 1)
    bd = (wi >= B_HEAD_DIM) == (wj >= B_HEAD_DIM)
    bd_ones = jnp.where(bd, 1.0, 0.0).astype(BF16)
    ti = lax.broadcasted_iota(jnp.int32, (L, L), 0)
    tj = lax.broadcasted_iota(jnp.int32, (L, L), 1)
    tri = jnp.where((tj >= ti) if reverse else (tj <= ti), 1.0, 0.0).astype(BF16)

    def stack_masked(x):
        return jnp.concatenate([jnp.where(head0, x, 0.0), jnp.where(head0, 0.0, x)], axis=0)

    def stack_dup(x):
        return jnp.concatenate([x, x], axis=0)

    subs = list(range(n_sub - 1, -1, -1) if reverse else range(n_sub))
    units = [(sub * L, slice(p * W, (p + 1) * W)) for sub in subs for p in range(B_PAIRS)]
    nu = len(units)

    def each(fn, *lists):
        return [fn(*args) for args in zip(*lists)]

    r_l = [r_ref[r0:r0 + L, sl].astype(F32) for r0, sl in units]
    k_l = [k_ref[r0:r0 + L, sl].astype(F32) for r0, sl in units]
    v_l = [v_ref[r0:r0 + L, sl].astype(F32) for r0, sl in units]
    lw_l = [-B_DECAY_SCALE * jax.nn.sigmoid(lw_ref[r0:r0 + L, sl]) for r0, sl in units]
    a_l = [jax.nn.sigmoid(a_ref[r0:r0 + L, sl]) for r0, sl in units]
    kkr_l = [k * kk_ref[:, sl] for k, (_, sl) in zip(k_l, units)]
    kd_l = [k * (1.0 + (a - 1.0) * ka_ref[:, sl]) for k, a, (_, sl) in zip(k_l, a_l, units)]
    sum_in = []
    for kkr, r, kd, (_, sl) in zip(kkr_l, r_l, kd_l, units):
        sum_in += _bf16_terms(kkr * kkr, 2) + _bf16_terms(r * kd * rk_ref[:, sl], 2)
    sums = _mxu(jnp.concatenate(sum_in, axis=0), bd_ones)
    ss_l = [sums[(4 * i) * L:(4 * i + 1) * L] + sums[(4 * i + 1) * L:(4 * i + 2) * L] for i in range(nu)]
    bsum_l = [sums[(4 * i + 2) * L:(4 * i + 3) * L] + sums[(4 * i + 3) * L:(4 * i + 4) * L] for i in range(nu)]
    lw_terms = []
    for lw in lw_l:
        lw_terms += _bf16_terms(lw, 2)
    csum = _mxu(tri, jnp.concatenate(lw_terms, axis=1))
    c_inc_l = [csum[:, (2 * i) * W:(2 * i + 1) * W] + csum[:, (2 * i + 1) * W:(2 * i + 2) * W] for i in range(nu)]
    kk_l = each(lambda kkr, ss: kkr / jnp.maximum(jnp.sqrt(ss), 1e-12), kkr_l, ss_l)
    bv_l = each(lambda kk, a: kk * a, kk_l, a_l)
    bonus_l = each(lambda bs, v: bs * v, bsum_l, v_l)
    c_mid_l = [c[L // 2:L // 2 + 1, :] for c in c_inc_l]
    tot_l = [(c[0:1, :] if reverse else c[L - 1:L, :]) for c in c_inc_l]
    x_all_l = each(lambda kk, r, c, lw: jnp.concatenate(
        [stack_masked(-kk * jnp.exp(c - lw)), stack_masked(r * jnp.exp(c))], axis=0), kk_l, r_l, c_inc_l, lw_l)
    xa_l = each(lambda x, cm: (x * jnp.exp(-cm)).astype(BF16), x_all_l, c_mid_l)
    x_all_l = [x.astype(BF16) for x in x_all_l]

    def decayed(bv, kd, e):
        return jnp.concatenate([stack_dup(bv * e), stack_dup(kd * e)], axis=0).astype(BF16)

    y_all_l = each(lambda bv, kd, c, cm: decayed(bv, kd, jnp.exp(cm - c)), bv_l, kd_l, c_inc_l, c_mid_l)
    y_end_l = each(lambda bv, kd, c, t: decayed(bv, kd, jnp.exp(t - c)), bv_l, kd_l, c_inc_l, tot_l)
    v_s_l = [stack_masked(v).astype(BF16) for v in v_l]

    a_all_l = each(lambda x, y: _mxu(x, y, _NT), xa_l, y_all_l)
    a_ab_l = [jnp.where(strict, a[:R, :R], 0.0) for a in a_all_l]
    a_ak_l = [jnp.where(strict, a[:R, R:], 0.0).astype(BF16) for a in a_all_l]
    a_r_l = [jnp.concatenate([jnp.where(incl, a[R:, :R], 0.0), jnp.where(incl, a[R:, R:], 0.0)],
                             axis=1).astype(BF16) for a in a_all_l]
    akv_l = each(lambda a, v: _mxu(a, v, _NN), a_ak_l, v_s_l)

    t_inv_l = [eye + a for a in a_ab_l]
    pw_l = [_mxu(a, a, _NN) for a in a_ab_l]
    for _ in range(4):
        both_l = each(lambda pw, t: _mxu(pw, jnp.concatenate([pw, t], axis=1), _NN), pw_l, t_inv_l)
        pw_l = [b[:, :R] for b in both_l]
        t_inv_l = each(lambda t, b: t + b[:, R:], t_inv_l, both_l)
    t_inv_l = each(lambda t, pw: (t + _mxu(pw, t, _NN)).astype(BF16), t_inv_l, pw_l)

    for i0 in range(0, nu, B_PAIRS):
        idx = list(range(i0, i0 + B_PAIRS))
        s_prev_l = [s_sc[p] for p in range(B_PAIRS)]
        xs_l = [_mxu(x_all_l[i], s, _NT) for i, s in zip(idx, s_prev_l)]
        u_l = [_mxu(t_inv_l[i], xs[:R] + akv_l[i], _NN) for i, xs in zip(idx, xs_l)]
        uv_l = [jnp.concatenate([u.astype(BF16), v_s_l[i]], axis=0) for i, u in zip(idx, u_l)]
        y_s_l = [xs[R:] + _mxu(a_r_l[i], uv, _NN) for i, xs, uv in zip(idx, xs_l, uv_l)]
        for i, y_s in zip(idx, y_s_l):
            r0, sl = units[i]
            y_ref[r0:r0 + L, sl] = y_s[:L] + y_s[L:] + bonus_l[i]
        s_new_l = [s * jnp.exp(tot_l[i]) + _mxu(uv, y_end_l[i], _TN)
                   for i, s, uv in zip(idx, s_prev_l, uv_l)]
        for p, s_new in enumerate(s_new_l):
            s_sc[p] = jnp.where(bd, s_new, 0.0)

    sf_ref[...] = s_sc[...]


def _rwkv_scan(r, k, v, lw, a, k_k, k_a, r_k, s0, *, seq_lens, reverse):
    cb = B_SCAN_ROWS
    nseq = len(seq_lens)
    rows = sum(seq_lens)
    blk, seq, first = _scan_schedule(seq_lens, cb, reverse)
    tile = pl.BlockSpec((cb, D_MODEL), lambda t, blk, seq, first: (blk[t], 0))
    vec = pl.BlockSpec((1, D_MODEL), lambda t, blk, seq, first: (0, 0))
    st = pl.BlockSpec((None, B_PAIRS, B_PAIR_W, B_PAIR_W), lambda t, blk, seq, first: (seq[t], 0, 0, 0))
    grid_spec = pltpu.PrefetchScalarGridSpec(
        num_scalar_prefetch=3,
        grid=(blk.shape[0],),
        in_specs=[tile, tile, tile, tile, tile, vec, vec, vec, st],
        out_specs=[tile, st],
        scratch_shapes=[pltpu.VMEM((B_PAIRS, B_PAIR_W, B_PAIR_W), F32)],
    )
    kern = functools.partial(_rwkv_scan_kernel, reverse=reverse, n_sub=cb // B_CHUNK)
    return pl.pallas_call(
        kern,
        grid_spec=grid_spec,
        out_shape=[
            jax.ShapeDtypeStruct((rows, D_MODEL), F32),
            jax.ShapeDtypeStruct((nseq, B_PAIRS, B_PAIR_W, B_PAIR_W), F32),
        ],
        compiler_params=_cparams("arbitrary"),
        name="rwkv_scan_bwd" if reverse else "rwkv_scan_fwd",
    )(blk, seq, first, r, k, v, lw, a, k_k, k_a, r_k, s0)


def _pair_blockdiag(s):
    n = s.shape[0]
    s = s.reshape(n, B_PAIRS, 2, B_HEAD_DIM, B_HEAD_DIM)
    z = jnp.zeros_like(s[:, :, 0])
    top = jnp.concatenate([s[:, :, 0], z], axis=-1)
    bot = jnp.concatenate([z, s[:, :, 1]], axis=-1)
    return jnp.concatenate([top, bot], axis=-2)


def _pair_unblock(sp):
    n = sp.shape[0]
    h0 = sp[:, :, :B_HEAD_DIM, :B_HEAD_DIM]
    h1 = sp[:, :, B_HEAD_DIM:, B_HEAD_DIM:]
    return jnp.stack([h0, h1], axis=2).reshape(n, B_HEADS, B_HEAD_DIM, B_HEAD_DIM)


TM = 256
TM_WIDE = 512
TM_MOE = 1024
MOE_SUB = 512
MOE_CAP = 160


def _mod_spec(layer, tm):
    per_group = GROUP_ROWS // tm
    return pl.BlockSpec((None, None, N_ADA, D_MODEL), lambda i, *_: (layer, i // per_group, 0, 0))


def _row_spec(tm, cols, col_block=0):
    return pl.BlockSpec((tm, cols), lambda i, *_: (i, col_block))


def _full_spec(shape):
    nd = len(shape)
    return pl.BlockSpec(shape, lambda i, *_: (0,) * nd)


def _split_specs(tm, cols, n_first):
    t_first = n_first // tm
    return (pl.BlockSpec((tm, cols), lambda i, *_: (jnp.minimum(i, t_first - 1), 0)),
            pl.BlockSpec((tm, cols), lambda i, *_: (jnp.maximum(i - t_first, 0), 0)))


def _in_proj_kernel(xp_ref, xs_ref, mod_ref, g_ref, w_ref, b_ref, wg_ref, bg_ref, z_ref, gate_ref, *, ctx_tiles):
    x = jnp.where(pl.program_id(0) < ctx_tiles, xp_ref[...], xs_ref[...])
    hn = _norm_mod(x, g_ref[...], mod_ref[1:2, :], mod_ref[0:1, :])
    z_ref[...] = _dot(hn.astype(BF16), w_ref[...]) + b_ref[...]
    gate_ref[...] = _mxu(hn, wg_ref[...], na=2, nb=2) + bg_ref[...]


def _in_proj(xp, xs, mod, g1, w_main, b_main, w_gate, b_gate):
    rows = xp.shape[0] + xs.shape[0]
    cols = w_main.shape[1]
    tm = TM_WIDE
    return pl.pallas_call(
        functools.partial(_in_proj_kernel, ctx_tiles=xp.shape[0] // tm),
        grid=(rows // tm,),
        in_specs=[
            *_split_specs(tm, D_MODEL, xp.shape[0]), _mod_spec(0, tm), _full_spec((1, D_MODEL)),
            _full_spec(w_main.shape), _full_spec(b_main.shape), _full_spec(w_gate.shape), _full_spec(b_gate.shape),
        ],
        out_specs=[_row_spec(tm, cols), _row_spec(tm, A_GATE_COLS)],
        out_shape=[jax.ShapeDtypeStruct((rows, cols), F32), jax.ShapeDtypeStruct((rows, A_GATE_COLS), F32)],
        compiler_params=_cparams("parallel"),
        name="mlstm_in_proj",
    )(xp, xs, mod, g1, w_main, b_main, w_gate, b_gate)


def _mlstm_out_ff_kernel(hf_ref, hb_ref, o_ref, xp_ref, xs_ref, mod_ref, modn_ref, gh_ref, wo_ref, g2_ref,
                         wgu_ref, wdn_ref, gn_ref, x_out_ref, hn_out_ref, *, ctx_tiles):
    x = jnp.where(pl.program_id(0) < ctx_tiles, xp_ref[...], xs_ref[...])
    hs = hf_ref[...] + hb_ref[...]
    parts = []
    for h in range(A_HEADS):
        seg = hs[:, h * A_DV:(h + 1) * A_DV]
        parts.append(seg * lax.rsqrt(jnp.mean(seg * seg, axis=-1, keepdims=True) + RMS_EPS))
    y = (jnp.concatenate(parts, axis=1) * gh_ref[...]) * jax.nn.sigmoid(o_ref[...])
    x1 = x + mod_ref[2:3, :] * _dot(y.astype(BF16), wo_ref[...])
    hn2 = _norm_mod(x1, g2_ref[...], mod_ref[4:5, :], mod_ref[3:4, :])
    gu = _dot(hn2.astype(BF16), wgu_ref[...])
    act = jax.nn.silu(gu[:, :D_FF]) * gu[:, D_FF:]
    x2 = x1 + mod_ref[5:6, :] * _dot(act.astype(BF16), wdn_ref[...])
    x_out_ref[...] = x2
    hn_out_ref[...] = _norm_mod(x2, gn_ref[...], modn_ref[1:2, :], modn_ref[0:1, :])


def _mlstm_out_ff(hf, hb, z, xp, xs, mod, g_head, w_out, g2, w_gu, w_down, g1_next):
    rows = xp.shape[0] + xs.shape[0]
    return pl.pallas_call(
        functools.partial(_mlstm_out_ff_kernel, ctx_tiles=xp.shape[0] // TM),
        grid=(rows // TM,),
        in_specs=[
            _row_spec(TM, A_V), _row_spec(TM, A_V), _row_spec(TM, A_V, 2), *_split_specs(TM, D_MODEL, xp.shape[0]),
            _mod_spec(0, TM), _mod_spec(1, TM), _full_spec((1, A_V)), _full_spec(w_out.shape),
            _full_spec((1, D_MODEL)), _full_spec(w_gu.shape), _full_spec(w_down.shape), _full_spec((1, D_MODEL)),
        ],
        out_specs=[_row_spec(TM, D_MODEL), _row_spec(TM, D_MODEL)],
        out_shape=[jax.ShapeDtypeStruct((rows, D_MODEL), F32)] * 2,
        compiler_params=_cparams("parallel"),
        name="mlstm_out_ff",
    )(hf, hb, z, xp, xs, mod, mod, g_head, w_out, g2, w_gu, w_down, g1_next)


def _token_shift(cur, up, dn, is_ctx, first, last):
    tm = cur.shape[0]
    q = D_MODEL // 4
    row = lax.broadcasted_iota(jnp.int32, (tm, q), 0)
    col = row % GRID_W
    prev = pltpu.roll(cur, 1, 0)
    nxt = pltpu.roll(cur, tm - 1, 0)
    q0 = jnp.where(jnp.where(is_ctx, row, col) == 0, 0.0, prev[:, :q])
    q1 = jnp.where(is_ctx, jnp.where(row == 0, 0.0, prev[:, q:2 * q]),
                   jnp.where(col == GRID_W - 1, 0.0, nxt[:, q:2 * q]))
    upper = jnp.concatenate([up[:, 2 * q:3 * q], cur[:tm - GRID_W, 2 * q:3 * q]], axis=0)
    upper = jnp.where(first & (row < GRID_W), 0.0, upper)
    q2 = jnp.where(is_ctx, jnp.where(row == tm - 1, 0.0, nxt[:, 2 * q:3 * q]), upper)
    lower = jnp.concatenate([cur[GRID_W:, 3 * q:], dn[:, 3 * q:]], axis=0)
    lower = jnp.where(last & (row >= tm - GRID_W), 0.0, lower)
    q3 = jnp.where(is_ctx, jnp.where(row == tm - 1, 0.0, nxt[:, 3 * q:]), lower)
    return jnp.concatenate([q0, q1, q2, q3], axis=1)


def _rwkv_in_kernel(hn_ref, up_ref, dn_ref, mu_ref, wrkv_ref, w1_ref, w2_ref, w0_ref, a1_ref, a2_ref, a0_ref,
                    g1_ref, g2_ref, r_ref, k_ref, v_ref, lw0_ref, lw1_ref, ia0_ref, ia1_ref, g_ref,
                    *, ctx_tiles, tiles_per_seq):
    i = pl.program_id(0)
    j = (i - ctx_tiles) % tiles_per_seq
    h = hn_ref[...]
    xx = _token_shift(h, up_ref[...], dn_ref[...], i < ctx_tiles, j == 0, j == tiles_per_seq - 1) - h

    def mix(c):
        return (h + xx * mu_ref[c:c + 1, :]).astype(BF16)

    r_ref[...] = _dot(mix(0), wrkv_ref[0]).astype(BF16)
    k_ref[...] = _dot(mix(1), wrkv_ref[1]).astype(BF16)
    v_ref[...] = _dot(mix(2), wrkv_ref[2]).astype(BF16)
    lane = lax.broadcasted_iota(jnp.int32, (h.shape[0], 2 * 64), 1)
    dir0 = lane < 64
    t = jnp.tanh(_dot(mix(3), w1_ref[...]))
    lw0_ref[...] = w0_ref[0:1, :] + _dot(jnp.where(dir0, t, 0.0).astype(BF16), w2_ref[...])
    lw1_ref[...] = w0_ref[1:2, :] + _dot(jnp.where(dir0, 0.0, t).astype(BF16), w2_ref[...])
    u = _dot(mix(4), a1_ref[...])
    ia0_ref[...] = a0_ref[0:1, :] + _dot(jnp.where(dir0, u, 0.0).astype(BF16), a2_ref[...])
    ia1_ref[...] = a0_ref[1:2, :] + _dot(jnp.where(dir0, 0.0, u).astype(BF16), a2_ref[...])
    g_ref[...] = _dot(jax.nn.sigmoid(_dot(mix(5), g1_ref[...])).astype(BF16), g2_ref[...]).astype(BF16)


def _rwkv_in(hn, mu, w_rkv, w1, w2, w0, a1, a2, a0, g1, g2, *, n_ctx, ctx_len, lat_len):
    rows = hn.shape[0]
    assert ctx_len == TM and lat_len % TM == 0 and TM % GRID_W == 0
    per_tile = TM // GRID_W
    n_halo = rows // GRID_W
    ins = [mu, w_rkv, w1, w2, w0, a1, a2, a0, g1, g2]
    kern = functools.partial(_rwkv_in_kernel, ctx_tiles=n_ctx // TM, tiles_per_seq=lat_len // TM)
    return pl.pallas_call(
        kern,
        grid=(rows // TM,),
        in_specs=[
            _row_spec(TM, D_MODEL),
            pl.BlockSpec((GRID_W, D_MODEL), lambda i: (jnp.maximum(i * per_tile - 1, 0), 0)),
            pl.BlockSpec((GRID_W, D_MODEL), lambda i: (jnp.minimum((i + 1) * per_tile, n_halo - 1), 0)),
        ] + [_full_spec(a.shape) for a in ins],
        out_specs=[_row_spec(TM, D_MODEL)] * 8,
        out_shape=[jax.ShapeDtypeStruct((rows, D_MODEL), dt) for dt in (BF16, BF16, BF16, F32, F32, F32, F32, BF16)],
        compiler_params=_cparams("parallel"),
        name="rwkv_in_proj",
    )(hn, hn, hn, *ins)


def _rwkv_out_kernel(yf_ref, yb_ref, g_ref, x_ref, mod_ref, lng_ref, lnb_ref, wo_ref, x_out_ref):
    y = yf_ref[...] + yb_ref[...]
    wi = lax.broadcasted_iota(jnp.int32, (B_PAIR_W, B_PAIR_W), 0)
    wj = lax.broadcasted_iota(jnp.int32, (B_PAIR_W, B_PAIR_W), 1)
    bd_mean = jnp.where((wi >= B_HEAD_DIM) == (wj >= B_HEAD_DIM), 1.0 / B_HEAD_DIM, 0.0)
    parts = []
    for p in range(B_PAIRS):
        seg = y[:, p * B_PAIR_W:(p + 1) * B_PAIR_W]
        cen = seg - _mxu(seg, bd_mean, na=3)
        var = _mxu(cen * cen, bd_mean, na=3)
        parts.append(cen * lax.rsqrt(var + B_LN_EPS))
    yn = jnp.concatenate(parts, axis=1) * lng_ref[...] + lnb_ref[...]
    out = _dot((yn * g_ref[...]).astype(BF16), wo_ref[...])
    x_out_ref[...] = x_ref[...] + mod_ref[2:3, :] * out


def _rwkv_out(yf, yb, g, x, mod, ln_g, ln_b, w_out):
    rows = x.shape[0]
    tm = TM_WIDE
    return pl.pallas_call(
        _rwkv_out_kernel,
        grid=(rows // tm,),
        in_specs=[_row_spec(tm, D_MODEL)] * 4 + [
            _mod_spec(1, tm), _full_spec((1, D_MODEL)), _full_spec((1, D_MODEL)), _full_spec(w_out.shape)],
        out_specs=_row_spec(tm, D_MODEL),
        out_shape=jax.ShapeDtypeStruct((rows, D_MODEL), F32),
        compiler_params=_cparams("parallel"),
        name="rwkv_out_proj",
    )(yf, yb, g, x, mod, ln_g, ln_b, w_out)


def _moe_kernel(x_ref, mod_ref, g2_ref, wr_ref, br_ref, upper_ref, wgu_ref, wdn_ref, gf_ref, out_p_ref, out_s_ref,
                hn_sc, sel_sc, cw_sc, rank_sc, acc_sc, *, ctx_tiles):
    e = pl.program_id(1)
    tm = x_ref.shape[0]
    n_sub = tm // MOE_SUB

    @pl.when(e == 0)
    def _():
        hn = _norm_mod(x_ref[...], g2_ref[...], mod_ref[4:5, :], mod_ref[3:4, :])
        hn_sc[...] = hn.astype(BF16)
        logits = _mxu(hn, wr_ref[...], na=3, nb=3) + br_ref[...]
        lane = lax.broadcasted_iota(jnp.int32, logits.shape, 1)
        m1 = jnp.max(logits, axis=1, keepdims=True)
        i1 = jnp.min(jnp.where(logits == m1, lane, ROUTER_COLS), axis=1, keepdims=True)
        rest = jnp.where(lane == i1, -jnp.inf, logits)
        m2 = jnp.max(rest, axis=1, keepdims=True)
        i2 = jnp.min(jnp.where(rest == m2, lane, ROUTER_COLS), axis=1, keepdims=True)
        e2 = jnp.exp(m2 - m1)
        den = 1.0 + e2
        comb = jnp.where(lane == i1, 1.0 / den, 0.0) + jnp.where(lane == i2, e2 / den, 0.0)
        sel = jnp.where((lane == i1) | (lane == i2), 1.0, 0.0)
        for s in range(n_sub):
            rows = slice(s * MOE_SUB, (s + 1) * MOE_SUB)
            sel_t = sel[rows, :].T[:N_EXPERTS, :]
            sel_sc[s] = sel_t
            cw_sc[s] = comb[rows, :].T[:N_EXPERTS, :]
            rank_sc[s] = _mxu(sel_t, upper_ref[...]).astype(jnp.int32)
        acc_sc[...] = jnp.zeros_like(acc_sc)

    for s in range(n_sub):
        rows = slice(s * MOE_SUB, (s + 1) * MOE_SUB)
        sel_row = sel_sc[s, pl.ds(e, 1), :]
        cw_row = cw_sc[s, pl.ds(e, 1), :]
        rank_row = rank_sc[s, pl.ds(e, 1), :]
        count = jnp.sum(sel_row).astype(jnp.int32)

        def evaluate(first_rank, cap):
            slot = lax.broadcasted_iota(jnp.int32, (cap, MOE_SUB), 0) + first_rank
            pick = jnp.where((rank_row == slot) & (sel_row > 0.5), 1.0, 0.0)
            pick_bf = pick.astype(BF16)
            xg = _mxu(pick_bf, hn_sc[rows, :])
            gu = _mxu(xg, wgu_ref[...])
            act = jax.nn.silu(gu[:, :D_FF_EXPERT]) * gu[:, D_FF_EXPERT:]
            ye = _mxu(act, wdn_ref[...])
            wj = jnp.sum(pick * cw_row, axis=1, keepdims=True)
            acc_sc[rows, :] += _mxu(pick_bf, wj * ye, _TN)

        def one_round(c, carry):
            evaluate(c * MOE_CAP, MOE_CAP)
            return carry

        lax.fori_loop(0, (count + MOE_CAP - 1) // MOE_CAP, one_round, 0)

    def final_norm():
        x4 = x_ref[...] + mod_ref[5:6, :] * acc_sc[...]
        return (x4 * lax.rsqrt(jnp.mean(x4 * x4, axis=-1, keepdims=True) + RMS_EPS)) * gf_ref[...]

    is_ctx = pl.program_id(0) < ctx_tiles

    @pl.when((e == N_EXPERTS - 1) & is_ctx)
    def _():
        out_p_ref[...] = final_norm()

    @pl.when((e == N_EXPERTS - 1) & jnp.logical_not(is_ctx))
    def _():
        out_s_ref[...] = final_norm()


def _moe(x, mod, g2, w_router, b_router, w_gu, w_down, g_final, *, n_ctx):
    rows = x.shape[0]
    tm = TM_MOE
    n_sub = tm // MOE_SUB
    upper = jnp.triu(jnp.ones((MOE_SUB, MOE_SUB), BF16), k=1)
    return pl.pallas_call(
        functools.partial(_moe_kernel, ctx_tiles=n_ctx // tm),
        grid=(rows // tm, N_EXPERTS),
        in_specs=[
            _row_spec(tm, D_MODEL), _mod_spec(1, tm), _full_spec((1, D_MODEL)),
            _full_spec(w_router.shape), _full_spec(b_router.shape), _full_spec(upper.shape),
            pl.BlockSpec((None, D_MODEL, 2 * D_FF_EXPERT), lambda i, e: (e, 0, 0)),
            pl.BlockSpec((None, D_FF_EXPERT, D_MODEL), lambda i, e: (e, 0, 0)),
            _full_spec((1, D_MODEL)),
        ],
        out_specs=list(_split_specs(tm, D_MODEL, n_ctx)),
        out_shape=[jax.ShapeDtypeStruct((n_ctx, D_MODEL), F32), jax.ShapeDtypeStruct((rows - n_ctx, D_MODEL), F32)],
        scratch_shapes=[
            pltpu.VMEM((tm, D_MODEL), BF16),
            pltpu.VMEM((n_sub, N_EXPERTS, MOE_SUB), F32),
            pltpu.VMEM((n_sub, N_EXPERTS, MOE_SUB), F32),
            pltpu.VMEM((n_sub, N_EXPERTS, MOE_SUB), jnp.int32),
            pltpu.VMEM((tm, D_MODEL), F32),
        ],
        compiler_params=_cparams("arbitrary", "arbitrary"),
        name="moe_final",
    )(x, mod, g2, w_router, b_router, upper, w_gu, w_down, g_final)


def _pad_cols(a, cols, value=0.0):
    return jnp.pad(a, ((0, 0), (0, cols - a.shape[1])), constant_values=value)


def kernel(x_prompt, x_sample, state_mlstm_C, state_mlstm_n, state_mlstm_m, state_rwkv_S, c, c_ctx, w_ada, b_ada, g_norm1, g_norm2, w_m_in, b_m_in, g_m_head, w_m_out, mu_rwkv, w_rkv, w_decay0, w_decay1, w_decay2, w_iclr0, w_iclr1, w_iclr2, w_gate1, w_gate2, k_k, k_a, r_k, g_ln_x, b_ln_x, w_rwkv_out, w_ff_gu, w_ff_down, w_router, b_router, w_exp_gu, w_exp_down, g_final):
    nb, seq, d = x_prompt.shape
    db, dseq, _ = x_sample.shape
    n_ctx = nb * seq
    n_lat = db * dseq
    assert d == D_MODEL and n_ctx == GROUP_ROWS and dseq == GROUP_ROWS

    xp = x_prompt.reshape(n_ctx, d)
    xs = x_sample.reshape(n_lat, d)
    cond = jnp.concatenate([c_ctx[None, :], c], axis=0)
    n_groups = cond.shape[0]
    cond = jnp.pad(cond, ((0, (-n_groups) % 8), (0, 0)))
    mod = _ada(cond, w_ada, b_ada).reshape(w_ada.shape[0], cond.shape[0], N_ADA, d)
    row = lambda a: a.reshape(1, -1)

    n_main = 2 * A_QK + 2 * A_V
    w_in = w_m_in[0]
    z, gates = _in_proj(xp, xs, mod, row(g_norm1[0]), w_in[:, :n_main].astype(BF16), row(b_m_in[0][:n_main]),
                        _pad_cols(w_in[:, n_main:], A_GATE_COLS), _pad_cols(row(b_m_in[0][n_main:]), A_GATE_COLS))
    gates_t = gates[:, :4 * A_HEADS].T
    seq_lens = (seq,) * nb + (dseq,) * db
    with_zero_ctx = lambda s: jnp.concatenate([jnp.zeros((nb,) + s.shape[1:], F32), s.astype(F32)], axis=0)
    h_dirs, ctx_state = [], []
    for dirn, rev in ((0, False), (1, True)):
        h_d, cf, nf, mf = _mlstm_scan(
            z, gates, gates_t, with_zero_ctx(state_mlstm_C[:, 0, dirn]), with_zero_ctx(state_mlstm_n[:, 0, dirn]),
            with_zero_ctx(jnp.broadcast_to(state_mlstm_m[:, 0, dirn][..., None], (db, A_HEADS, A_DV))),
            seq_lens=seq_lens, reverse=rev)
        h_dirs.append(h_d)
        ctx_state.append((cf[:nb], nf[:nb], mf[:nb, :, 0]))
    new_c = jnp.stack([s[0] for s in ctx_state], axis=1)[:, None]
    new_n = jnp.stack([s[1] for s in ctx_state], axis=1)[:, None]
    new_m = jnp.stack([s[2] for s in ctx_state], axis=1)[:, None]
    x, hn = _mlstm_out_ff(h_dirs[0], h_dirs[1], z, xp, xs, mod, row(g_m_head[0]), w_m_out[0].astype(BF16),
                          row(g_norm2[0]), w_ff_gu[0].astype(BF16), w_ff_down[0].astype(BF16), row(g_norm1[1]))

    cat_cols = lambda w: jnp.concatenate([w[0], w[1]], axis=1).astype(BF16)
    cat_rows = lambda w: jnp.concatenate([w[0], w[1]], axis=0).astype(BF16)
    r, k, v, lw0, lw1, ia0, ia1, g = _rwkv_in(
        hn, mu_rwkv[0], w_rkv[0].astype(BF16), cat_cols(w_decay1[0]), cat_rows(w_decay2[0]), w_decay0[0],
        cat_cols(w_iclr1[0]), cat_rows(w_iclr2[0]), w_iclr0[0], w_gate1[0].astype(BF16), w_gate2[0].astype(BF16),
        n_ctx=n_ctx, ctx_len=seq, lat_len=dseq)
    y_dirs, ctx_s = [], []
    for dirn, rev, lw, ia in ((0, False, lw0, ia0), (1, True, lw1, ia1)):
        y_d, sf = _rwkv_scan(r, k, v, lw, ia, row(k_k[0]), row(k_a[0]), row(r_k[0]),
                             with_zero_ctx(_pair_blockdiag(state_rwkv_S[:, 0, dirn])), seq_lens=seq_lens, reverse=rev)
        y_dirs.append(y_d)
        ctx_s.append(_pair_unblock(sf[:nb]))
    new_s = jnp.stack(ctx_s, axis=1)[:, None]
    x = _rwkv_out(y_dirs[0], y_dirs[1], g, x, mod, row(g_ln_x[0]), row(b_ln_x[0]), w_rwkv_out[0].astype(BF16))
    y_p, y_s = _moe(x, mod, row(g_norm2[1]), _pad_cols(w_router[0], ROUTER_COLS),
                    _pad_cols(row(b_router[0]), ROUTER_COLS, NEG_BIG), w_exp_gu[0].astype(BF16),
                    w_exp_down[0].astype(BF16), row(g_final), n_ctx=n_ctx)
    return (y_p.reshape(nb, seq, d), y_s.reshape(db, dseq, d), new_c, new_n, new_m, new_s)
```
